```python
import math
import jax, jax.numpy as jnp
from jax import lax
import numpy as np

D_MODEL = 4096
BATCH = 2
SEQ = 8192
DEPTH = 4

GRID_W = 64
CTX_LEN = 256
MIX_W = D_MODEL
CONV_W = D_MODEL // 2
CONV_K = 3
MLA_HEADS = 16
QK_NOPE = 128
ROPE_DIM = 64
ROPE_AXIS_DIM = ROPE_DIM // 2
ROPE_FREQ = ROPE_AXIS_DIM // 2
ROPE_BASE = 10000.0
V_DIM = 128
MLA_W = MLA_HEADS * V_DIM
Q_LORA = 768
KV_LORA = 512
Q_BLOCK = 128
ATTN_SCALE = 1.0 / math.sqrt(QK_NOPE + ROPE_DIM)
DEEPNORM_ALPHA = (2.0 * DEPTH) ** 0.25
DEEPNORM_BETA = (8.0 * DEPTH) ** (-0.25)
EPS = 1e-6

IN_SECTIONS = [CONV_W, CONV_W, CONV_W, CONV_W, Q_LORA, KV_LORA, ROPE_DIM, MLA_W]
IN_W = int(sum(IN_SECTIONS))
SPLIT_IDX = [int(v) for v in np.cumsum(IN_SECTIONS)[:-1]]

kernel_name = "hybrid_conv_mla_prefix_dit_block"


def _rms(x, g):
    xf = x.astype(jnp.float32)
    y = xf * lax.rsqrt(jnp.mean(xf * xf, axis=-1, keepdims=True) + EPS)
    return y.astype(x.dtype) * g


def _post_ln(z, g, b):
    zf = z.astype(jnp.float32)
    mu = jnp.mean(zf, axis=-1, keepdims=True)
    var = jnp.mean(jnp.square(zf - mu), axis=-1, keepdims=True)
    return ((zf - mu) * lax.rsqrt(var + EPS)).astype(z.dtype) * g + b


def _rope_2d(x, cos, sin):
    shp = x.shape
    xr = x.reshape(shp[:-1] + (2, 2, ROPE_FREQ))
    extra = x.ndim - 3
    bshape = (shp[1],) + (1,) * extra + (2, 1, ROPE_FREQ)
    c = cos.reshape(bshape).astype(x.dtype)
    s = sin.reshape(bshape).astype(x.dtype)
    rot = jnp.stack([-xr[..., 1, :], xr[..., 0, :]], axis=-2)
    return (xr * c + rot * s).reshape(shp)


def _project(h, w_in, q_g, w_uq, kv_g, w_ukv):
    p = h @ w_in
    u, bg, cg, zc, cq, ckv, kr, zm = jnp.split(p, SPLIT_IDX, axis=-1)
    lead = h.shape[:-1]
    q = (_rms(cq, q_g) @ w_uq).reshape(lead + (MLA_HEADS, QK_NOPE + ROPE_DIM))
    kv = (_rms(ckv, kv_g) @ w_ukv).reshape(lead + (MLA_HEADS, QK_NOPE + V_DIM))
    q_nope, q_rope = q[..., :QK_NOPE], q[..., QK_NOPE:]
    k_nope, v = kv[..., :QK_NOPE], kv[..., QK_NOPE:]
    return (u, bg, cg, zc), (q_nope, q_rope, k_nope, kr, v), zm


def _conv_branch(u, bg, cg, zc, conv_w):
    v = cg * u
    vp = jnp.pad(v, ((0, 0), (1, 1), (0, 0)))
    y = vp[:, :-2] * conv_w[0] + vp[:, 1:-1] * conv_w[1] + vp[:, 2:] * conv_w[2]
    return bg * y * jax.nn.silu(zc)


def _attend(qn, qr, kn, kr, v):
    s = jnp.einsum('bqhd,bkhd->bhqk', qn, kn) + jnp.einsum('bqhr,bkr->bhqk', qr, kr)
    p = jax.nn.softmax(s.astype(jnp.float32) * ATTN_SCALE, axis=-1)
    return jnp.einsum('bhqk,bkhd->bqhd', p.astype(v.dtype), v)


def _merge_out(y_conv, y_attn, zm, w_out):
    b, l = y_attn.shape[0], y_attn.shape[1]
    y_mla = y_attn.reshape(b, l, MLA_W) * jax.nn.silu(zm)
    return jnp.concatenate([y_conv, y_mla], axis=-1) @ w_out


def setup_inputs(seed: int = 0) -> dict:
    key = jax.random.key(seed)
    ks = jax.random.split(key, 16)
    f32 = jnp.float32
    x = jax.random.normal(ks[0], (BATCH, SEQ, D_MODEL), f32)
    c = jax.random.normal(ks[1], (BATCH, D_MODEL), f32)
    ctx = jax.random.normal(ks[2], (BATCH, CTX_LEN, D_MODEL), f32)
    c_ctx = jax.random.normal(ks[3], (D_MODEL,), f32)
    w_ada = jax.random.normal(ks[4], (DEPTH, D_MODEL, 3 * D_MODEL), f32) * (0.5 * D_MODEL ** -0.5)
    b_ada = 0.01 * jax.random.normal(ks[5], (DEPTH, 3 * D_MODEL), f32)
    w_in = jax.random.normal(ks[6], (DEPTH, D_MODEL, IN_W), f32) * D_MODEL ** -0.5
    conv_w = jax.random.normal(ks[7], (DEPTH, CONV_K, CONV_W), f32) * CONV_K ** -0.5
    q_norm_g = 1.0 + 0.02 * jax.random.normal(ks[8], (DEPTH, Q_LORA), f32)
    w_uq = jax.random.normal(ks[9], (DEPTH, Q_LORA, MLA_HEADS * (QK_NOPE + ROPE_DIM)), f32) * Q_LORA ** -0.5
    kv_norm_g = 1.0 + 0.02 * jax.random.normal(ks[10], (DEPTH, KV_LORA), f32)
    w_ukv = jax.random.normal(ks[11], (DEPTH, KV_LORA, MLA_HEADS * (QK_NOPE + V_DIM)), f32) * KV_LORA ** -0.5
    w_out = jax.random.normal(ks[12], (DEPTH, MIX_W, D_MODEL), f32) * (DEEPNORM_BETA * MIX_W ** -0.5)
    ln_g = 1.0 + 0.02 * jax.random.normal(ks[13], (DEPTH, D_MODEL), f32)
    ln_b = 0.01 * jax.random.normal(ks[14], (DEPTH, D_MODEL), f32)
    return {"x": x, "c": c, "ctx": ctx, "c_ctx": c_ctx, "w_ada": w_ada, "b_ada": b_ada,
            "w_in": w_in, "conv_w": conv_w, "q_norm_g": q_norm_g, "w_uq": w_uq,
            "kv_norm_g": kv_norm_g, "w_ukv": w_ukv, "w_out": w_out, "ln_g": ln_g, "ln_b": ln_b}


def reference(x, c, ctx, c_ctx, w_ada, b_ada, w_in, conv_w, q_norm_g, w_uq,
              kv_norm_g, w_ukv, w_out, ln_g, ln_b):
    B, S, D = x.shape
    ROWS = S // GRID_W
    n_blk = S // Q_BLOCK

    row = jnp.repeat(jnp.arange(ROWS, dtype=jnp.float32), GRID_W)
    col = jnp.tile(jnp.arange(GRID_W, dtype=jnp.float32), ROWS)
    pos = jnp.stack([row, col], axis=-1)
    inv_freq = ROPE_BASE ** (-(jnp.arange(ROPE_FREQ, dtype=jnp.float32) * 2.0) / ROPE_AXIS_DIM)
    ang = pos[:, :, None] * inv_freq
    cos, sin = jnp.cos(ang), jnp.sin(ang)

    for l in range(DEPTH):
        last = l == DEPTH - 1
        mod = jax.nn.silu(c) @ w_ada[l] + b_ada[l]
        shift, scale, gate = jnp.split(mod, 3, axis=-1)
        mod_c = jax.nn.silu(c_ctx) @ w_ada[l] + b_ada[l]
        shift_c, scale_c, gate_c = jnp.split(mod_c, 3, axis=-1)

        h = x * (1.0 + scale[:, None, :]) + shift[:, None, :]
        hc = ctx * (1.0 + scale_c) + shift_c

        conv_in, mla_in, zm = _project(h, w_in[l], q_norm_g[l], w_uq[l], kv_norm_g[l], w_ukv[l])
        conv_in_c, mla_in_c, zm_c = _project(hc, w_in[l], q_norm_g[l], w_uq[l], kv_norm_g[l], w_ukv[l])

        qn, qr, kn, kr, v = mla_in
        qr = _rope_2d(qr, cos, sin)
        kr = _rope_2d(kr, cos, sin)
        qn_c, qr_c, kn_c, kr_c, v_c = mla_in_c

        kn_all = jnp.concatenate([kn, kn_c], axis=1)
        kr_all = jnp.concatenate([kr, kr_c], axis=1)
        v_all = jnp.concatenate([v, v_c], axis=1)
        qn_b = qn.reshape(B, n_blk, Q_BLOCK, MLA_HEADS, QK_NOPE).transpose(1, 0, 2, 3, 4)
        qr_b = qr.reshape(B, n_blk, Q_BLOCK, MLA_HEADS, ROPE_DIM).transpose(1, 0, 2, 3, 4)
        o = lax.map(lambda a: _attend(a[0], a[1], kn_all, kr_all, v_all), (qn_b, qr_b))
        y_attn = o.transpose(1, 0, 2, 3, 4).reshape(B, S, MLA_HEADS, V_DIM)

        y_conv = _conv_branch(*conv_in, conv_w[l])
        out = _merge_out(y_conv, y_attn, zm, w_out[l])

        if not last:
            y_attn_c = _attend(qn_c, qr_c, kn_c, kr_c, v_c)
            y_conv_c = _conv_branch(*conv_in_c, conv_w[l])
            out_c = _merge_out(y_conv_c, y_attn_c, zm_c, w_out[l])
            ctx = _post_ln(DEEPNORM_ALPHA * ctx + gate_c * out_c, ln_g[l], ln_b[l])

        x = _post_ln(DEEPNORM_ALPHA * x + gate[:, None, :] * out, ln_g[l], ln_b[l])

    return x
```

```python
import functools
import math

import jax
import jax.numpy as jnp
from jax import lax
from jax.experimental import pallas as pl
from jax.experimental.pallas import tpu as pltpu

F32 = jnp.float32
BF16 = jnp.bfloat16

HEAD_DIM = 128
ROPE_DIM = 64
ROPE_FREQ = ROPE_DIM // 4
ROPE_BASE = 10000.0
GRID_W = 64
QK_PAD = 2 * HEAD_DIM
EPS = 1e-6
ATTN_SCALE = 1.0 / math.sqrt(HEAD_DIM + ROPE_DIM)
LANE = 128
SUBLANE = 8
VMEM_LIMIT = 56 * 1024 * 1024


def _params(n_axes, vmem=VMEM_LIMIT):
    return pltpu.CompilerParams(dimension_semantics=("arbitrary",) * n_axes,
                                vmem_limit_bytes=vmem)


def _tile(n, prefs):
    for t in prefs:
        if n % t == 0:
            return t
    return n


def _silu(x):
    return x / (1.0 + jnp.exp(-x))


def _mod_kernel(a_ref, w_ref, b_ref, o_ref):
    a = _silu(a_ref[...]).astype(BF16)
    o_ref[...] = jnp.dot(a, w_ref[...].astype(BF16), preferred_element_type=F32) + b_ref[...]


def _modulation(cc, w_ada, b_ada):
    depth, d, n = w_ada.shape
    tn = _tile(n, (512, 256, 128))
    return pl.pallas_call(
        _mod_kernel,
        grid=(depth, n // tn),
        in_specs=[pl.BlockSpec((SUBLANE, d), lambda l, j: (0, 0)),
                  pl.BlockSpec((None, d, tn), lambda l, j: (l, 0, j)),
                  pl.BlockSpec((None, 1, tn), lambda l, j: (l, 0, j))],
        out_specs=pl.BlockSpec((None, SUBLANE, tn), lambda l, j: (l, 0, j)),
        out_shape=jax.ShapeDtypeStruct((depth, SUBLANE, n), F32),
        compiler_params=_params(2),
        name="adaln_mod",
    )(cc, w_ada, b_ada.reshape(depth, 1, n))


def _row_spec(mods, layer, part, group_of_tile):
    d = mods.shape[-1]
    return pl.BlockSpec((None, None, None, 1, d),
                        lambda i: (layer, group_of_tile(i), part, 0, 0))


def _modulate_kernel(x_ref, sh_ref, sc_ref, h_ref):
    h_ref[...] = (x_ref[...] * (1.0 + sc_ref[...]) + sh_ref[...]).astype(h_ref.dtype)


def _modulate(x_all, mods, layer, group_of, tm):
    m, d = x_all.shape
    gof = lambda i: group_of(i, tm)
    return pl.pallas_call(
        _modulate_kernel,
        grid=(m // tm,),
        in_specs=[pl.BlockSpec((tm, d), lambda i: (i, 0)),
                  _row_spec(mods, layer, 0, gof),
                  _row_spec(mods, layer, 1, gof)],
        out_specs=pl.BlockSpec((tm, d), lambda i: (i, 0)),
        out_shape=jax.ShapeDtypeStruct((m, d), BF16),
        compiler_params=_params(1),
        name="modulate",
    )(x_all, mods, mods)


def _mm_kernel(a_ref, b_ref, o_ref):
    o_ref[...] = jnp.dot(a_ref[...], b_ref[...], preferred_element_type=F32).astype(o_ref.dtype)


def _matmul(a, w, layer, out_dtype, rows, tm, tn, name):
    k = a.shape[1]
    n = w.shape[2]
    return pl.pallas_call(
        _mm_kernel,
        grid=(n // tn, rows // tm),
        in_specs=[pl.BlockSpec((tm, k), lambda j, i: (i, 0)),
                  pl.BlockSpec((None, k, tn), lambda j, i: (layer, 0, j))],
        out_specs=pl.BlockSpec((tm, tn), lambda j, i: (i, j)),
        out_shape=jax.ShapeDtypeStruct((rows, n), out_dtype),
        compiler_params=_params(2),
        name=name,
    )(a, w)


def _mm2_kernel(a1_ref, a2_ref, b1_ref, b2_ref, o_ref):
    acc = jnp.dot(a1_ref[...], b1_ref[...], preferred_element_type=F32)
    acc += jnp.dot(a2_ref[...], b2_ref[...], preferred_element_type=F32)
    o_ref[...] = acc.astype(o_ref.dtype)


def _matmul2(a1, a2, w1, w2, layer, rows, tm, tn, name):
    k = a1.shape[1]
    n = w1.shape[2]
    return pl.pallas_call(
        _mm2_kernel,
        grid=(n // tn, rows // tm),
        in_specs=[pl.BlockSpec((tm, k), lambda j, i: (i, 0)),
                  pl.BlockSpec((tm, k), lambda j, i: (i, 0)),
                  pl.BlockSpec((None, k, tn), lambda j, i: (layer, 0, j)),
                  pl.BlockSpec((None, k, tn), lambda j, i: (layer, 0, j))],
        out_specs=pl.BlockSpec((tm, tn), lambda j, i: (i, j)),
        out_shape=jax.ShapeDtypeStruct((rows, n), F32),
        compiler_params=_params(2),
        name=name,
    )(a1, a2, w1, w2)


def _rope_half(x, cos, sin):
    return x * cos + pltpu.roll(x, ROPE_DIM, axis=1) * sin


def _rms(x, g):
    return (x * lax.rsqrt(jnp.mean(x * x, axis=-1, keepdims=True) + EPS)) * g


def _up_kernel(p_ref, cos_ref, sin_ref, gq_ref, gkv_ref, wq_ref, wkv_ref,
               q_ref, k_ref, v_ref, *, ql, kvl, heads):
    cos = cos_ref[...]
    sin = sin_ref[...]
    cq = _rms(p_ref[:, :ql], gq_ref[...]).astype(BF16)
    ckv = _rms(p_ref[:, ql:ql + kvl], gkv_ref[...]).astype(BF16)
    krp = _rope_half(p_ref[:, ql + kvl:], cos, sin).astype(BF16)
    qf = jnp.dot(cq, wq_ref[...], preferred_element_type=F32)
    kvf = jnp.dot(ckv, wkv_ref[...], preferred_element_type=F32)
    for h in range(heads):
        lo = h * QK_PAD
        q_ref[:, lo:lo + HEAD_DIM] = qf[:, lo:lo + HEAD_DIM].astype(BF16)
        q_ref[:, lo + HEAD_DIM:lo + QK_PAD] = _rope_half(
            qf[:, lo + HEAD_DIM:lo + QK_PAD], cos, sin).astype(BF16)
        k_ref[:, lo:lo + HEAD_DIM] = kvf[:, h * HEAD_DIM:(h + 1) * HEAD_DIM].astype(BF16)
        k_ref[:, lo + HEAD_DIM:lo + QK_PAD] = krp
    v_ref[...] = kvf[:, heads * HEAD_DIM:].astype(BF16)


def _up_project(p_lora, cos_t, sin_t, gq, gkv, wq, wkv, layer, heads, tm):
    m, pw = p_lora.shape
    ql, kvl = gq.shape[-1], gkv.shape[-1]
    depth = wq.shape[0]
    const = lambda i: (layer, 0, 0)
    return pl.pallas_call(
        functools.partial(_up_kernel, ql=ql, kvl=kvl, heads=heads),
        grid=(m // tm,),
        in_specs=[pl.BlockSpec((tm, pw), lambda i: (i, 0)),
                  pl.BlockSpec((tm, LANE), lambda i: (i, 0)),
                  pl.BlockSpec((tm, LANE), lambda i: (i, 0)),
                  pl.BlockSpec((None, 1, ql), const),
                  pl.BlockSpec((None, 1, kvl), const),
                  pl.BlockSpec((None, ql, heads * QK_PAD), const),
                  pl.BlockSpec((None, kvl, 2 * heads * HEAD_DIM), const)],
        out_specs=[pl.BlockSpec((tm, heads * QK_PAD), lambda i: (i, 0)),
                   pl.BlockSpec((tm, heads * QK_PAD), lambda i: (i, 0)),
                   pl.BlockSpec((tm, heads * HEAD_DIM), lambda i: (i, 0))],
        out_shape=[jax.ShapeDtypeStruct((m, heads * QK_PAD), BF16),
                   jax.ShapeDtypeStruct((m, heads * QK_PAD), BF16),
                   jax.ShapeDtypeStruct((m, heads * HEAD_DIM), BF16)],
        compiler_params=_params(1),
        name="lora_up",
    )(p_lora, cos_t, sin_t, gq.reshape(depth, 1, ql), gkv.reshape(depth, 1, kvl), wq, wkv)


def _scores(q, k):
    return lax.dot_general(q, k, (((1,), (1,)), ((), ())), preferred_element_type=F32) * ATTN_SCALE


def _attn_kernel(q_ref, kl_ref, kc_ref, vl_ref, vc_ref, o_ref, m_ref, l_ref, acc_ref, *, tk):
    q = q_ref[...]
    m_ref[...] = jnp.full(m_ref.shape, -jnp.inf, F32)
    l_ref[...] = jnp.zeros(l_ref.shape, F32)
    acc_ref[...] = jnp.zeros(acc_ref.shape, F32)

    def update(k, v):
        s = _scores(q, k)
        m_old = m_ref[...]
        m_new = jnp.maximum(m_old, jnp.max(s, axis=-1, keepdims=True))
        alpha = jnp.exp(m_old - m_new)
        p = jnp.exp(s - m_new)
        l_ref[...] = alpha * l_ref[...] + jnp.sum(p, axis=-1, keepdims=True)
        acc_ref[...] = alpha * acc_ref[...] + jnp.dot(p.astype(BF16), v, preferred_element_type=F32)
        m_ref[...] = m_new

    def body(c, carry):
        off = pl.multiple_of(c * tk, tk)
        update(kl_ref[pl.ds(off, tk), :], vl_ref[pl.ds(off, tk), :])
        return carry

    lax.fori_loop(0, kl_ref.shape[0] // tk, body, 0)
    update(kc_ref[...], vc_ref[...])
    o_ref[...] = (acc_ref[...] / l_ref[...]).astype(o_ref.dtype)


def _attend_latent(q, k, v, batch, seq, ctx_len, heads, tq, tk):
    n_q = seq // tq
    ctx0 = batch * seq // ctx_len
    return pl.pallas_call(
        functools.partial(_attn_kernel, tk=tk),
        grid=(batch, heads, n_q),
        in_specs=[pl.BlockSpec((tq, QK_PAD), lambda b, h, i: (b * n_q + i, h)),
                  pl.BlockSpec((seq, QK_PAD), lambda b, h, i: (b, h)),
                  pl.BlockSpec((ctx_len, QK_PAD), lambda b, h, i: (ctx0 + b, h)),
                  pl.BlockSpec((seq, HEAD_DIM), lambda b, h, i: (b, h)),
                  pl.BlockSpec((ctx_len, HEAD_DIM), lambda b, h, i: (ctx0 + b, h))],
        out_specs=pl.BlockSpec((tq, HEAD_DIM), lambda b, h, i: (b * n_q + i, h)),
        out_shape=jax.ShapeDtypeStruct((batch * seq, heads * HEAD_DIM), BF16),
        scratch_shapes=[pltpu.VMEM((tq, 1), F32), pltpu.VMEM((tq, 1), F32),
                        pltpu.VMEM((tq, HEAD_DIM), F32)],
        compiler_params=_params(3),
        name="attn_latent",
    )(q, k, k, v, v)


def _attn_ctx_kernel(q_ref, k_ref, v_ref, o_ref):
    s = _scores(q_ref[...], k_ref[...])
    p = jnp.exp(s - jnp.max(s, axis=-1, keepdims=True))
    o = jnp.dot(p.astype(BF16), v_ref[...], preferred_element_type=F32)
    o_ref[...] = (o / jnp.sum(p, axis=-1, keepdims=True)).astype(o_ref.dtype)


def _attend_ctx(q, k, v, batch, seq, ctx_len, heads):
    ctx0 = batch * seq // ctx_len
    return pl.pallas_call(
        _attn_ctx_kernel,
        grid=(batch, heads),
        in_specs=[pl.BlockSpec((ctx_len, QK_PAD), lambda b, h: (ctx0 + b, h)),
                  pl.BlockSpec((ctx_len, QK_PAD), lambda b, h: (ctx0 + b, h)),
                  pl.BlockSpec((ctx_len, HEAD_DIM), lambda b, h: (ctx0 + b, h))],
        out_specs=pl.BlockSpec((ctx_len, HEAD_DIM), lambda b, h: (b, h)),
        out_shape=jax.ShapeDtypeStruct((batch * ctx_len, heads * HEAD_DIM), BF16),
        compiler_params=_params(2),
        name="attn_ctx",
    )(q, k, v)


def _merge_kernel(u_ref, bg_ref, cg_ref, zc_ref, zm_ref, up_ref, cp_ref, un_ref, cn_ref,
                  ya_ref, cw_ref, yc_ref, ym_ref, *, tm, lat_rows, seq, ctx_len):
    i = pl.program_id(0)
    row0 = i * tm
    row1 = row0 + tm
    is_start = jnp.where(row0 < lat_rows, row0 % seq == 0, (row0 - lat_rows) % ctx_len == 0)
    is_end = jnp.where(row1 <= lat_rows, row1 % seq == 0, (row1 - lat_rows) % ctx_len == 0)
    v = cg_ref[...].astype(F32) * u_ref[...].astype(F32)
    v_before = (cp_ref[SUBLANE - 1:SUBLANE, :].astype(F32) * up_ref[SUBLANE - 1:SUBLANE, :].astype(F32))
    v_after = cn_ref[0:1, :].astype(F32) * un_ref[0:1, :].astype(F32)
    v_before = jnp.where(is_start, 0.0, v_before)
    v_after = jnp.where(is_end, 0.0, v_after)
    row = lax.broadcasted_iota(jnp.int32, v.shape, 0)
    v_prev = jnp.where(row == 0, v_before, pltpu.roll(v, 1, axis=0))
    v_next = jnp.where(row == tm - 1, v_after, pltpu.roll(v, tm - 1, axis=0))
    cw = cw_ref[...]
    y = v_prev * cw[0:1, :] + v * cw[1:2, :] + v_next * cw[2:3, :]
    yc_ref[...] = (bg_ref[...].astype(F32) * y * _silu(zc_ref[...].astype(F32))).astype(yc_ref.dtype)
    ym_ref[...] = (ya_ref[...].astype(F32) * _silu(zm_ref[...].astype(F32))).astype(ym_ref.dtype)


def _merge(p_mix, y_attn, conv_w, layer, rows, lat_rows, seq, ctx_len, tm, tc):
    m_all = p_mix.shape[0]
    cw = conv_w.shape[-1]
    nc = cw // tc
    hb = tm // SUBLANE
    last_hb = m_all // SUBLANE - 1
    main = lambda s: pl.BlockSpec((tm, tc), lambda i, j: (i, s * nc + j))
    prev = lambda s: pl.BlockSpec((SUBLANE, tc), lambda i, j: (jnp.maximum(i * hb - 1, 0), s * nc + j))
    nxt = lambda s: pl.BlockSpec((SUBLANE, tc), lambda i, j: (jnp.minimum((i + 1) * hb, last_hb), s * nc + j))
    out = pl.BlockSpec((tm, tc), lambda i, j: (i, j))
    return pl.pallas_call(
        functools.partial(_merge_kernel, tm=tm, lat_rows=lat_rows, seq=seq, ctx_len=ctx_len),
        grid=(rows // tm, nc),
        in_specs=[main(0), main(1), main(2), main(3), main(4),
                  prev(0), prev(2), nxt(0), nxt(2),
                  pl.BlockSpec((tm, tc), lambda i, j: (i, j)),
                  pl.BlockSpec((None, 3, tc), lambda i, j: (layer, 0, j))],
        out_specs=[out, out],
        out_shape=[jax.ShapeDtypeStruct((rows, cw), BF16), jax.ShapeDtypeStruct((rows, cw), BF16)],
        compiler_params=_params(2),
        name="conv_merge",
    )(p_mix, p_mix, p_mix, p_mix, p_mix, p_mix, p_mix, p_mix, p_mix, y_attn, conv_w)


def _layernorm(z, g, b):
    mu = jnp.mean(z, axis=-1, keepdims=True)
    zc = z - mu
    var = jnp.mean(zc * zc, axis=-1, keepdims=True)
    return zc * lax.rsqrt(var + EPS) * g + b


def _postln_next_kernel(x_ref, o_ref, gate_ref, g_ref, b_ref, sh_ref, sc_ref, xn_ref, h_ref, *, alpha):
    xn = _layernorm(alpha * x_ref[...] + gate_ref[...] * o_ref[...], g_ref[...], b_ref[...])
    xn_ref[...] = xn
    h_ref[...] = (xn * (1.0 + sc_ref[...]) + sh_ref[...]).astype(h_ref.dtype)


def _postln_last_kernel(x_ref, o_ref, gate_ref, g_ref, b_ref, xn_ref, *, alpha):
    xn_ref[...] = _layernorm(alpha * x_ref[...] + gate_ref[...] * o_ref[...], g_ref[...], b_ref[...])


def _postln(x_all, out, mods, ln_g, ln_b, layer, rows, group_of, tm, alpha, last):
    d = x_all.shape[1]
    depth = ln_g.shape[0]
    gof = lambda i: group_of(i, tm)
    tile = pl.BlockSpec((tm, d), lambda i: (i, 0))
    vec = pl.BlockSpec((None, 1, d), lambda i: (layer, 0, 0))
    in_specs = [tile, tile, _row_spec(mods, layer, 2, gof), vec, vec]
    args = [x_all, out, mods, ln_g.reshape(depth, 1, d), ln_b.reshape(depth, 1, d)]
    if last:
        return pl.pallas_call(
            functools.partial(_postln_last_kernel, alpha=alpha),
            grid=(rows // tm,), in_specs=in_specs, out_specs=tile,
            out_shape=jax.ShapeDtypeStruct((rows, d), F32),
            compiler_params=_params(1), name="postln_last",
        )(*args)
    in_specs += [_row_spec(mods, layer + 1, 0, gof), _row_spec(mods, layer + 1, 1, gof)]
    args += [mods, mods]
    return pl.pallas_call(
        functools.partial(_postln_next_kernel, alpha=alpha),
        grid=(rows // tm,), in_specs=in_specs, out_specs=[tile, tile],
        out_shape=[jax.ShapeDtypeStruct((rows, d), F32), jax.ShapeDtypeStruct((rows, d), BF16)],
        compiler_params=_params(1), name="postln_next",
    )(*args)


def _rot_cols(w):
    shp = w.shape
    wr = w.reshape(shp[:-1] + (2, 2, ROPE_FREQ))
    return jnp.stack([-wr[..., 1, :], wr[..., 0, :]], axis=-2).reshape(shp)


def _rope_tables(batch, seq, ctx_rows):
    rows = seq // GRID_W
    row = jnp.repeat(jnp.arange(rows, dtype=F32), GRID_W)
    col = jnp.tile(jnp.arange(GRID_W, dtype=F32), rows)
    pos = jnp.stack([row, col], axis=-1)
    inv_freq = ROPE_BASE ** (-(jnp.arange(ROPE_FREQ, dtype=F32) * 2.0) / (ROPE_DIM // 2))
    ang = pos[:, :, None] * inv_freq
    expand = lambda t: jnp.broadcast_to(t[:, :, None, :], (seq, 2, 2, ROPE_FREQ)).reshape(seq, ROPE_DIM)
    cos = jnp.concatenate([jnp.tile(expand(jnp.cos(ang)), (batch, 1)), jnp.ones((ctx_rows, ROPE_DIM), F32)])
    sin = jnp.concatenate([jnp.tile(expand(jnp.sin(ang)), (batch, 1)), jnp.zeros((ctx_rows, ROPE_DIM), F32)])
    pad = lambda t: jnp.pad(t, ((0, 0), (0, LANE - ROPE_DIM)))
    return pad(cos), pad(sin)


def kernel(x, c, ctx, c_ctx, w_ada, b_ada, w_in, conv_w, q_norm_g, w_uq, kv_norm_g, w_ukv, w_out, ln_g, ln_b):
    batch, seq, d = x.shape
    ctx_len = ctx.shape[1]
    depth = w_ada.shape[0]
    cw = conv_w.shape[-1]
    heads = (w_out.shape[1] - cw) // HEAD_DIM
    ql, kvl = q_norm_g.shape[-1], kv_norm_g.shape[-1]
    lat_rows, ctx_rows = batch * seq, batch * ctx_len
    m_all = lat_rows + ctx_rows
    alpha = (2.0 * depth) ** 0.25

    sec = [cw, cw, cw, cw, ql, kvl, ROPE_DIM, cw]
    offs = [0]
    for s in sec:
        offs.append(offs[-1] + s)
    col = lambda k: w_in[:, :, offs[k]:offs[k + 1]]
    w_mix = jnp.concatenate([col(0), col(1), col(2), col(3), col(7)], axis=-1).astype(BF16)
    w_lora = jnp.concatenate([col(4), col(5), col(6), _rot_cols(col(6))], axis=-1).astype(BF16)
    wq = w_uq.reshape(depth, ql, heads, HEAD_DIM + ROPE_DIM)
    wq = jnp.concatenate([wq, _rot_cols(wq[..., HEAD_DIM:])], axis=-1).reshape(depth, ql, heads * QK_PAD).astype(BF16)
    wkv = w_ukv.reshape(depth, kvl, heads, 2 * HEAD_DIM)
    wkv = jnp.concatenate([wkv[..., :HEAD_DIM].reshape(depth, kvl, heads * HEAD_DIM),
                           wkv[..., HEAD_DIM:].reshape(depth, kvl, heads * HEAD_DIM)], axis=-1).astype(BF16)
    wo_conv = w_out[:, :cw, :].astype(BF16)
    wo_mla = w_out[:, cw:, :].astype(BF16)
    cos_t, sin_t = _rope_tables(batch, seq, ctx_rows)

    cc = jnp.zeros((SUBLANE, d), F32).at[:batch].set(c).at[batch].set(c_ctx)
    mods = _modulation(cc, w_ada, b_ada)[:, :batch + 1].reshape(depth, batch + 1, 3, 1, d)
    group_of = lambda i, tm: jnp.minimum((i * tm) // seq, batch)

    tm = _tile(math.gcd(seq, ctx_rows), (512, 256, 128))
    tm_small = _tile(math.gcd(seq, ctx_len), (256, 128))
    x_all = jnp.concatenate([x.reshape(lat_rows, d), ctx.reshape(ctx_rows, d)], axis=0)
    h = _modulate(x_all, mods, 0, group_of, tm)

    for layer in range(depth):
        last = layer == depth - 1
        rows = lat_rows if last else m_all
        p_mix = _matmul(h, w_mix, layer, BF16, rows, tm, _tile(5 * cw, (1024, 512, 256, 128)), "in_proj_mix")
        p_lora = _matmul(h, w_lora, layer, F32, m_all, tm, w_lora.shape[-1], "in_proj_lora")
        q, k, v = _up_project(p_lora, cos_t, sin_t, q_norm_g, kv_norm_g, wq, wkv, layer, heads, tm_small)
        y_attn = _attend_latent(q, k, v, batch, seq, ctx_len, heads,
                                _tile(seq, (512, 256, 128)), _tile(seq, (512, 256, 128)))
        if not last:
            y_attn = jnp.concatenate([y_attn, _attend_ctx(q, k, v, batch, seq, ctx_len, heads)], axis=0)
        y_conv, y_mla = _merge(p_mix, y_attn, conv_w, layer, rows, lat_rows, seq, ctx_len,
                               tm_small, _tile(cw, (512, 256, 128)))
        out = _matmul2(y_conv, y_mla, wo_conv, wo_mla, layer, rows, tm, _tile(d, (1024, 512, 256, 128)), "out_proj")
        if last:
            x_all = _postln(x_all, out, mods, ln_g, ln_b, layer, rows, group_of, tm_small, alpha, True)
        else:
            x_all, h = _postln(x_all, out, mods, ln_g, ln_b, layer, rows, group_of, tm_small, alpha, False)
    return x_all.reshape(batch, seq, d)
```

```python
import functools
import math

import jax
import jax.numpy as jnp
from jax import lax
from jax.experimental import pallas as pl
from jax.experimental.pallas import tpu as pltpu

F32 = jnp.float32
BF16 = jnp.bfloat16

HEAD_DIM = 128
ROPE_DIM = 64
ROPE_FREQ = ROPE_DIM // 4
ROPE_BASE = 10000.0
GRID_W = 64
QK_PAD = 2 * HEAD_DIM
EPS = 1e-6
ATTN_SCALE = 1.0 / math.sqrt(HEAD_DIM + ROPE_DIM)
Q_PRESCALE = ATTN_SCALE * math.log2(math.e)
LANE = 128
SUBLANE = 8
VMEM_LIMIT = 56 * 1024 * 1024


def _params(n_axes, vmem=VMEM_LIMIT):
    return pltpu.CompilerParams(dimension_semantics=("arbitrary",) * n_axes,
                                vmem_limit_bytes=vmem)


def _tile(n, prefs):
    for t in prefs:
        if n % t == 0:
            return t
    return n


def _silu(x):
    return x / (1.0 + jnp.exp(-x))


def _mod_kernel(a_ref, w_ref, b_ref, o_ref):
    a = _silu(a_ref[...]).astype(BF16)
    o_ref[...] = jnp.dot(a, w_ref[...].astype(BF16), preferred_element_type=F32) + b_ref[...]


def _modulation(cc, w_ada, b_ada):
    depth, d, n = w_ada.shape
    tn = _tile(n, (512, 256, 128))
    return pl.pallas_call(
        _mod_kernel,
        grid=(depth, n // tn),
        in_specs=[pl.BlockSpec((SUBLANE, d), lambda l, j: (0, 0)),
                  pl.BlockSpec((None, d, tn), lambda l, j: (l, 0, j)),
                  pl.BlockSpec((None, 1, tn), lambda l, j: (l, 0, j))],
        out_specs=pl.BlockSpec((None, SUBLANE, tn), lambda l, j: (l, 0, j)),
        out_shape=jax.ShapeDtypeStruct((depth, SUBLANE, n), F32),
        compiler_params=_params(2),
        name="adaln_mod",
    )(cc, w_ada, b_ada.reshape(depth, 1, n))


def _row_spec(mods, layer, part, group_of_tile):
    d = mods.shape[-1]
    return pl.BlockSpec((None, None, None, 1, d),
                        lambda i: (layer, group_of_tile(i), part, 0, 0))


def _modulate_kernel(x_ref, sh_ref, sc_ref, h_ref):
    h_ref[...] = (x_ref[...] * (1.0 + sc_ref[...]) + sh_ref[...]).astype(h_ref.dtype)


def _modulate(x_all, mods, layer, group_of, tm):
    m, d = x_all.shape
    gof = lambda i: group_of(i, tm)
    return pl.pallas_call(
        _modulate_kernel,
        grid=(m // tm,),
        in_specs=[pl.BlockSpec((tm, d), lambda i: (i, 0)),
                  _row_spec(mods, layer, 0, gof),
                  _row_spec(mods, layer, 1, gof)],
        out_specs=pl.BlockSpec((tm, d), lambda i: (i, 0)),
        out_shape=jax.ShapeDtypeStruct((m, d), BF16),
        compiler_params=_params(1),
        name="modulate",
    )(x_all, mods, mods)


def _mm_kernel(a_ref, b_ref, o_ref):
    o_ref[...] = jnp.dot(a_ref[...], b_ref[...], preferred_element_type=F32).astype(o_ref.dtype)


def _matmul(a, w, layer, out_dtype, rows, tm, tn, name):
    k = a.shape[1]
    n = w.shape[2]
    return pl.pallas_call(
        _mm_kernel,
        grid=(n // tn, rows // tm),
        in_specs=[pl.BlockSpec((tm, k), lambda j, i: (i, 0)),
                  pl.BlockSpec((None, k, tn), lambda j, i: (layer, 0, j))],
        out_specs=pl.BlockSpec((tm, tn), lambda j, i: (i, j)),
        out_shape=jax.ShapeDtypeStruct((rows, n), out_dtype),
        compiler_params=_params(2),
        name=name,
    )(a, w)


def _mm2_kernel(a1_ref, a2_ref, b1_ref, b2_ref, o_ref):
    acc = jnp.dot(a1_ref[...], b1_ref[...], preferred_element_type=F32)
    acc += jnp.dot(a2_ref[...], b2_ref[...], preferred_element_type=F32)
    o_ref[...] = acc.astype(o_ref.dtype)


def _matmul2(a1, a2, w1, w2, layer, rows, tm, tn, name):
    k = a1.shape[1]
    n = w1.shape[2]
    return pl.pallas_call(
        _mm2_kernel,
        grid=(n // tn, rows // tm),
        in_specs=[pl.BlockSpec((tm, k), lambda j, i: (i, 0)),
                  pl.BlockSpec((tm, k), lambda j, i: (i, 0)),
                  pl.BlockSpec((None, k, tn), lambda j, i: (layer, 0, j)),
                  pl.BlockSpec((None, k, tn), lambda j, i: (layer, 0, j))],
        out_specs=pl.BlockSpec((tm, tn), lambda j, i: (i, j)),
        out_shape=jax.ShapeDtypeStruct((rows, n), F32),
        compiler_params=_params(2),
        name=name,
    )(a1, a2, w1, w2)


def _rope_half(x, cos, sin):
    return x * cos + pltpu.roll(x, ROPE_DIM, axis=1) * sin


def _rms(x, g):
    return (x * lax.rsqrt(jnp.mean(x * x, axis=-1, keepdims=True) + EPS)) * g


def _dot_nt(a, b):
    return lax.dot_general(a, b, (((1,), (1,)), ((), ())), preferred_element_type=F32)


def _up_kernel(p_ref, cos_ref, sin_ref, cos_t_ref, sin_t_ref, gq_ref, gkv_ref, wq_t_ref, wk_ref, wv_t_ref,
               q_t_ref, k_ref, v_t_ref, *, ql, kvl, heads):
    cq = _rms(p_ref[:, :ql], gq_ref[...]).astype(BF16)
    ckv = _rms(p_ref[:, ql:ql + kvl], gkv_ref[...]).astype(BF16)
    krp = _rope_half(p_ref[:, ql + kvl:], cos_ref[...], sin_ref[...]).astype(BF16)
    cos_t = cos_t_ref[...] * Q_PRESCALE
    sin_t = sin_t_ref[...] * Q_PRESCALE
    q_t = _dot_nt(wq_t_ref[...], cq)
    kn = jnp.dot(ckv, wk_ref[...], preferred_element_type=F32)
    for h in range(heads):
        lo = h * QK_PAD
        hi = q_t[lo + HEAD_DIM:lo + QK_PAD, :]
        q_t_ref[lo:lo + HEAD_DIM, :] = (q_t[lo:lo + HEAD_DIM, :] * Q_PRESCALE).astype(BF16)
        q_t_ref[lo + HEAD_DIM:lo + QK_PAD, :] = (
            hi * cos_t + pltpu.roll(hi, ROPE_DIM, axis=0) * sin_t).astype(BF16)
        k_ref[:, lo:lo + HEAD_DIM] = kn[:, h * HEAD_DIM:(h + 1) * HEAD_DIM].astype(BF16)
        k_ref[:, lo + HEAD_DIM:lo + QK_PAD] = krp
    v_t_ref[...] = _dot_nt(wv_t_ref[...], ckv).astype(BF16)


def _up_project(p_lora, tables, gq, gkv, wq_t, wk, wv_t, layer, heads, tm):
    m, pw = p_lora.shape
    ql, kvl = gq.shape[-1], gkv.shape[-1]
    depth = wq_t.shape[0]
    cos_r, sin_r, cos_t, sin_t = tables
    const = lambda i: (layer, 0, 0)
    return pl.pallas_call(
        functools.partial(_up_kernel, ql=ql, kvl=kvl, heads=heads),
        grid=(m // tm,),
        in_specs=[pl.BlockSpec((tm, pw), lambda i: (i, 0)),
                  pl.BlockSpec((tm, LANE), lambda i: (i, 0)),
                  pl.BlockSpec((tm, LANE), lambda i: (i, 0)),
                  pl.BlockSpec((LANE, tm), lambda i: (0, i)),
                  pl.BlockSpec((LANE, tm), lambda i: (0, i)),
                  pl.BlockSpec((None, 1, ql), const),
                  pl.BlockSpec((None, 1, kvl), const),
                  pl.BlockSpec((None, heads * QK_PAD, ql), const),
                  pl.BlockSpec((None, kvl, heads * HEAD_DIM), const),
                  pl.BlockSpec((None, heads * HEAD_DIM, kvl), const)],
        out_specs=[pl.BlockSpec((heads * QK_PAD, tm), lambda i: (0, i)),
                   pl.BlockSpec((tm, heads * QK_PAD), lambda i: (i, 0)),
                   pl.BlockSpec((None, heads * HEAD_DIM, tm), lambda i: (i, 0, 0))],
        out_shape=[jax.ShapeDtypeStruct((heads * QK_PAD, m), BF16),
                   jax.ShapeDtypeStruct((m, heads * QK_PAD), BF16),
                   jax.ShapeDtypeStruct((m // tm, heads * HEAD_DIM, tm), BF16)],
        compiler_params=_params(1),
        name="lora_up",
    )(p_lora, cos_r, sin_r, cos_t, sin_t, gq.reshape(depth, 1, ql), gkv.reshape(depth, 1, kvl), wq_t, wk, wv_t)


def _attn_kernel(q_t_ref, kl_ref, kc_ref, vl_ref, vc_ref, o_ref,
                 s0_ref, s1_ref, p0_ref, p1_ref, a0_ref, a1_ref, m_ref, l_ref, acc_ref, *, tk):
    n_lat = kl_ref.shape[0] // tk
    n = n_lat + 1
    s_buf, p_buf, a_buf = (s0_ref, s1_ref), (p0_ref, p1_ref), (a0_ref, a1_ref)
    q_t = q_t_ref[...]

    def k_chunk(c):
        if isinstance(c, int) and c == n_lat:
            return kc_ref[...]
        return kl_ref[pl.ds(pl.multiple_of(c * tk, tk), tk), :]

    def v_chunk(c):
        if isinstance(c, int) and c == n_lat:
            return vc_ref[0]
        return vl_ref[c]

    def scores(c, slot):
        s_buf[slot][...] = jnp.dot(k_chunk(c), q_t, preferred_element_type=F32)

    def softmax(slot):
        s = s_buf[slot][...]
        m_old = m_ref[...]
        m_new = jnp.maximum(m_old, jnp.max(s, axis=0, keepdims=True))
        alpha = jnp.exp2(m_old - m_new)
        p = jnp.exp2(s - m_new)
        l_ref[...] = alpha * l_ref[...] + jnp.sum(p, axis=0, keepdims=True)
        m_ref[...] = m_new
        a_buf[slot][...] = alpha
        p_buf[slot][...] = p.astype(BF16)

    def values(c, slot):
        acc_ref[...] = a_buf[slot][...] * acc_ref[...] + jnp.dot(
            v_chunk(c), p_buf[slot][...], preferred_element_type=F32)

    def stage(c, slot, with_scores=True, with_values=True):
        if with_scores:
            scores(c + 1, 1 - slot)
        softmax(slot)
        if with_values:
            values(c - 1, 1 - slot)

    m_ref[...] = jnp.full(m_ref.shape, -jnp.inf, F32)
    l_ref[...] = jnp.zeros(l_ref.shape, F32)
    acc_ref[...] = jnp.zeros(acc_ref.shape, F32)
    scores(0, 0)
    stage(0, 0, with_values=False)
    n_pairs = (n - 3) // 2

    def pair(j, carry):
        c = 1 + 2 * j
        stage(c, 1)
        stage(c + 1, 0)
        return carry

    lax.fori_loop(0, n_pairs, pair, 0)
    for c in range(1 + 2 * n_pairs, n):
        stage(c, c % 2, with_scores=c + 1 < n)
    values(n - 1, (n - 1) % 2)
    o_t = acc_ref[...] / l_ref[...]
    o_ref[...] = o_t.T.astype(o_ref.dtype)


def _attend_latent(q_t, k, v_t, batch, seq, ctx_len, heads, tq, tk):
    n_q = seq // tq
    ctx0 = batch * seq // ctx_len
    n_lat = seq // tk
    return pl.pallas_call(
        functools.partial(_attn_kernel, tk=tk),
        grid=(batch, heads, n_q),
        in_specs=[pl.BlockSpec((QK_PAD, tq), lambda b, h, i: (h, b * n_q + i)),
                  pl.BlockSpec((seq, QK_PAD), lambda b, h, i: (b, h)),
                  pl.BlockSpec((ctx_len, QK_PAD), lambda b, h, i: (ctx0 + b, h)),
                  pl.BlockSpec((n_lat, HEAD_DIM, tk), lambda b, h, i: (b, h, 0)),
                  pl.BlockSpec((ctx_len // tk, HEAD_DIM, tk), lambda b, h, i: (ctx0 + b, h, 0))],
        out_specs=pl.BlockSpec((tq, HEAD_DIM), lambda b, h, i: (b * n_q + i, h)),
        out_shape=jax.ShapeDtypeStruct((batch * seq, heads * HEAD_DIM), BF16),
        scratch_shapes=[pltpu.VMEM((tk, tq), F32), pltpu.VMEM((tk, tq), F32),
                        pltpu.VMEM((tk, tq), BF16), pltpu.VMEM((tk, tq), BF16),
                        pltpu.VMEM((1, tq), F32), pltpu.VMEM((1, tq), F32),
                        pltpu.VMEM((1, tq), F32), pltpu.VMEM((1, tq), F32),
                        pltpu.VMEM((HEAD_DIM, tq), F32)],
        compiler_params=_params(3),
        name="attn_latent",
    )(q_t, k, k, v_t, v_t)


def _attn_ctx_kernel(q_t_ref, k_ref, v_t_ref, o_ref):
    s = jnp.dot(k_ref[...], q_t_ref[...], preferred_element_type=F32)
    p = jnp.exp2(s - jnp.max(s, axis=0, keepdims=True))
    o_t = jnp.dot(v_t_ref[0], p.astype(BF16), preferred_element_type=F32)
    o_ref[...] = (o_t / jnp.sum(p, axis=0, keepdims=True)).T.astype(o_ref.dtype)


def _attend_ctx(q_t, k, v_t, batch, seq, ctx_len, heads):
    ctx0 = batch * seq // ctx_len
    return pl.pallas_call(
        _attn_ctx_kernel,
        grid=(batch, heads),
        in_specs=[pl.BlockSpec((QK_PAD, ctx_len), lambda b, h: (h, ctx0 + b)),
                  pl.BlockSpec((ctx_len, QK_PAD), lambda b, h: (ctx0 + b, h)),
                  pl.BlockSpec((1, HEAD_DIM, ctx_len), lambda b, h: (ctx0 + b, h, 0))],
        out_specs=pl.BlockSpec((ctx_len, HEAD_DIM), lambda b, h: (b, h)),
        out_shape=jax.ShapeDtypeStruct((batch * ctx_len, heads * HEAD_DIM), BF16),
        compiler_params=_params(2),
        name="attn_ctx",
    )(q_t, k, v_t)


def _merge_kernel(u_ref, bg_ref, cg_ref, zc_ref, zm_ref, up_ref, cp_ref, un_ref, cn_ref,
                  ya_ref, cw_ref, yc_ref, ym_ref, *, tm, lat_rows, seq, ctx_len):
    i = pl.program_id(0)
    row0 = i * tm
    row1 = row0 + tm
    is_start = jnp.where(row0 < lat_rows, row0 % seq == 0, (row0 - lat_rows) % ctx_len == 0)
    is_end = jnp.where(row1 <= lat_rows, row1 % seq == 0, (row1 - lat_rows) % ctx_len == 0)
    v = cg_ref[...].astype(F32) * u_ref[...].astype(F32)
    v_before = (cp_ref[SUBLANE - 1:SUBLANE, :].astype(F32) * up_ref[SUBLANE - 1:SUBLANE, :].astype(F32))
    v_after = cn_ref[0:1, :].astype(F32) * un_ref[0:1, :].astype(F32)
    v_before = jnp.where(is_start, 0.0, v_before)
    v_after = jnp.where(is_end, 0.0, v_after)
    row = lax.broadcasted_iota(jnp.int32, v.shape, 0)
    v_prev = jnp.where(row == 0, v_before, pltpu.roll(v, 1, axis=0))
    v_next = jnp.where(row == tm - 1, v_after, pltpu.roll(v, tm - 1, axis=0))
    cw = cw_ref[...]
    y = v_prev * cw[0:1, :] + v * cw[1:2, :] + v_next * cw[2:3, :]
    yc_ref[...] = (bg_ref[...].astype(F32) * y * _silu(zc_ref[...].astype(F32))).astype(yc_ref.dtype)
    ym_ref[...] = (ya_ref[...].astype(F32) * _silu(zm_ref[...].astype(F32))).astype(ym_ref.dtype)


def _merge(p_mix, y_attn, conv_w, layer, rows, lat_rows, seq, ctx_len, tm, tc):
    m_all = p_mix.shape[0]
    cw = conv_w.shape[-1]
    nc = cw // tc
    hb = tm // SUBLANE
    last_hb = m_all // SUBLANE - 1
    main = lambda s: pl.BlockSpec((tm, tc), lambda i, j: (i, s * nc + j))
    prev = lambda s: pl.BlockSpec((SUBLANE, tc), lambda i, j: (jnp.maximum(i * hb - 1, 0), s * nc + j))
    nxt = lambda s: pl.BlockSpec((SUBLANE, tc), lambda i, j: (jnp.minimum((i + 1) * hb, last_hb), s * nc + j))
    out = pl.BlockSpec((tm, tc), lambda i, j: (i, j))
    return pl.pallas_call(
        functools.partial(_merge_kernel, tm=tm, lat_rows=lat_rows, seq=seq, ctx_len=ctx_len),
        grid=(rows // tm, nc),
        in_specs=[main(0), main(1), main(2), main(3), main(4),
                  prev(0), prev(2), nxt(0), nxt(2),
                  pl.BlockSpec((tm, tc), lambda i, j: (i, j)),
                  pl.BlockSpec((None, 3, tc), lambda i, j: (layer, 0, j))],
        out_specs=[out, out],
        out_shape=[jax.ShapeDtypeStruct((rows, cw), BF16), jax.ShapeDtypeStruct((rows, cw), BF16)],
        compiler_params=_params(2),
        name="conv_merge",
    )(p_mix, p_mix, p_mix, p_mix, p_mix, p_mix, p_mix, p_mix, p_mix, y_attn, conv_w)


def _layernorm(z, g, b):
    mu = jnp.mean(z, axis=-1, keepdims=True)
    zc = z - mu
    var = jnp.mean(zc * zc, axis=-1, keepdims=True)
    return zc * lax.rsqrt(var + EPS) * g + b


def _postln_next_kernel(x_ref, o_ref, gate_ref, g_ref, b_ref, sh_ref, sc_ref, xn_ref, h_ref, *, alpha):
    xn = _layernorm(alpha * x_ref[...] + gate_ref[...] * o_ref[...], g_ref[...], b_ref[...])
    xn_ref[...] = xn
    h_ref[...] = (xn * (1.0 + sc_ref[...]) + sh_ref[...]).astype(h_ref.dtype)


def _postln_last_kernel(x_ref, o_ref, gate_ref, g_ref, b_ref, xn_ref, *, alpha):
    xn_ref[...] = _layernorm(alpha * x_ref[...] + gate_ref[...] * o_ref[...], g_ref[...], b_ref[...])


def _postln(x_all, out, mods, ln_g, ln_b, layer, rows, group_of, tm, alpha, last):
    d = x_all.shape[1]
    depth = ln_g.shape[0]
    gof = lambda i: group_of(i, tm)
    tile = pl.BlockSpec((tm, d), lambda i: (i, 0))
    vec = pl.BlockSpec((None, 1, d), lambda i: (layer, 0, 0))
    in_specs = [tile, tile, _row_spec(mods, layer, 2, gof), vec, vec]
    args = [x_all, out, mods, ln_g.reshape(depth, 1, d), ln_b.reshape(depth, 1, d)]
    if last:
        return pl.pallas_call(
            functools.partial(_postln_last_kernel, alpha=alpha),
            grid=(rows // tm,), in_specs=in_specs, out_specs=tile,
            out_shape=jax.ShapeDtypeStruct((rows, d), F32),
            compiler_params=_params(1), name="postln_last",
        )(*args)
    in_specs += [_row_spec(mods, layer + 1, 0, gof), _row_spec(mods, layer + 1, 1, gof)]
    args += [mods, mods]
    return pl.pallas_call(
        functools.partial(_postln_next_kernel, alpha=alpha),
        grid=(rows // tm,), in_specs=in_specs, out_specs=[tile, tile],
        out_shape=[jax.ShapeDtypeStruct((rows, d), F32), jax.ShapeDtypeStruct((rows, d), BF16)],
        compiler_params=_params(1), name="postln_next",
    )(*args)


def _rot_cols(w):
    shp = w.shape
    wr = w.reshape(shp[:-1] + (2, 2, ROPE_FREQ))
    return jnp.stack([-wr[..., 1, :], wr[..., 0, :]], axis=-2).reshape(shp)


def _rope_tables(batch, seq, ctx_rows):
    rows = seq // GRID_W
    row = jnp.repeat(jnp.arange(rows, dtype=F32), GRID_W)
    col = jnp.tile(jnp.arange(GRID_W, dtype=F32), rows)
    pos = jnp.stack([row, col], axis=-1)
    inv_freq = ROPE_BASE ** (-(jnp.arange(ROPE_FREQ, dtype=F32) * 2.0) / (ROPE_DIM // 2))
    ang = pos[:, :, None] * inv_freq
    expand = lambda t: jnp.broadcast_to(t[:, :, None, :], (seq, 2, 2, ROPE_FREQ)).reshape(seq, ROPE_DIM)
    cos = jnp.concatenate([jnp.tile(expand(jnp.cos(ang)), (batch, 1)), jnp.ones((ctx_rows, ROPE_DIM), F32)])
    sin = jnp.concatenate([jnp.tile(expand(jnp.sin(ang)), (batch, 1)), jnp.zeros((ctx_rows, ROPE_DIM), F32)])
    pad = lambda t: jnp.pad(t, ((0, 0), (0, LANE - ROPE_DIM)))
    cos, sin = pad(cos), pad(sin)
    return cos, sin, cos.T, sin.T


def kernel(x, c, ctx, c_ctx, w_ada, b_ada, w_in, conv_w, q_norm_g, w_uq, kv_norm_g, w_ukv, w_out, ln_g, ln_b):
    batch, seq, d = x.shape
    ctx_len = ctx.shape[1]
    depth = w_ada.shape[0]
    cw = conv_w.shape[-1]
    heads = (w_out.shape[1] - cw) // HEAD_DIM
    ql, kvl = q_norm_g.shape[-1], kv_norm_g.shape[-1]
    lat_rows, ctx_rows = batch * seq, batch * ctx_len
    m_all = lat_rows + ctx_rows
    alpha = (2.0 * depth) ** 0.25

    sec = [cw, cw, cw, cw, ql, kvl, ROPE_DIM, cw]
    offs = [0]
    for s in sec:
        offs.append(offs[-1] + s)
    col = lambda k: w_in[:, :, offs[k]:offs[k + 1]]
    w_mix = jnp.concatenate([col(0), col(1), col(2), col(3), col(7)], axis=-1).astype(BF16)
    w_lora = jnp.concatenate([col(4), col(5), col(6), _rot_cols(col(6))], axis=-1).astype(BF16)
    wq = w_uq.reshape(depth, ql, heads, HEAD_DIM + ROPE_DIM)
    wq = jnp.concatenate([wq, _rot_cols(wq[..., HEAD_DIM:])], axis=-1).reshape(depth, ql, heads * QK_PAD)
    wq_t = jnp.swapaxes(wq, 1, 2).astype(BF16)
    wkv = w_ukv.reshape(depth, kvl, heads, 2 * HEAD_DIM)
    wk = wkv[..., :HEAD_DIM].reshape(depth, kvl, heads * HEAD_DIM).astype(BF16)
    wv_t = jnp.swapaxes(wkv[..., HEAD_DIM:].reshape(depth, kvl, heads * HEAD_DIM), 1, 2).astype(BF16)
    wo_conv = w_out[:, :cw, :].astype(BF16)
    wo_mla = w_out[:, cw:, :].astype(BF16)
    tables = _rope_tables(batch, seq, ctx_rows)

    cc = jnp.zeros((SUBLANE, d), F32).at[:batch].set(c).at[batch].set(c_ctx)
    mods = _modulation(cc, w_ada, b_ada)[:, :batch + 1].reshape(depth, batch + 1, 3, 1, d)
    group_of = lambda i, tm: jnp.minimum((i * tm) // seq, batch)

    tm = _tile(math.gcd(seq, ctx_rows), (512, 256, 128))
    tm_small = _tile(math.gcd(seq, ctx_len), (256, 128))
    assert tm_small == ctx_len, "the context keys must form exactly one key chunk"
    x_all = jnp.concatenate([x.reshape(lat_rows, d), ctx.reshape(ctx_rows, d)], axis=0)
    h = _modulate(x_all, mods, 0, group_of, tm)

    for layer in range(depth):
        last = layer == depth - 1
        rows = lat_rows if last else m_all
        p_mix = _matmul(h, w_mix, layer, BF16, rows, tm, _tile(5 * cw, (1024, 512, 256, 128)), "in_proj_mix")
        p_lora = _matmul(h, w_lora, layer, F32, m_all, tm, w_lora.shape[-1], "in_proj_lora")
        q_t, k, v_t = _up_project(p_lora, tables, q_norm_g, kv_norm_g, wq_t, wk, wv_t, layer, heads, tm_small)
        y_attn = _attend_latent(q_t, k, v_t, batch, seq, ctx_len, heads, _tile(seq, (512, 256, 128)), tm_small)
        if not last:
            y_attn = jnp.concatenate([y_attn, _attend_ctx(q_t, k, v_t, batch, seq, ctx_len, heads)], axis=0)
        y_conv, y_mla = _merge(p_mix, y_attn, conv_w, layer, rows, lat_rows, seq, ctx_len,
                               tm_small, _tile(cw, (512, 256, 128)))
        out = _matmul2(y_conv, y_mla, wo_conv, wo_mla, layer, rows, tm, _tile(d, (1024, 512, 256, 128)), "out_proj")
        if last:
            x_all = _postln(x_all, out, mods, ln_g, ln_b, layer, rows, group_of, tm_small, alpha, True)
        else:
            x_all, h = _postln(x_all, out, mods, ln_g, ln_b, layer, rows, group_of, tm_small, alpha, False)
    return x_all.reshape(batch, seq, d)
```

```python
import functools
import math

import jax
import jax.numpy as jnp
from jax import lax
from jax.experimental import pallas as pl
from jax.experimental.pallas import tpu as pltpu

F32 = jnp.float32
BF16 = jnp.bfloat16

HEAD_DIM = 128
ROPE_DIM = 64
ROPE_FREQ = ROPE_DIM // 4
ROPE_BASE = 10000.0
GRID_W = 64
QK_PAD = 2 * HEAD_DIM
EPS = 1e-6
ATTN_SCALE = 1.0 / math.sqrt(HEAD_DIM + ROPE_DIM)
Q_PRESCALE = ATTN_SCALE * math.log2(math.e)
ATTN_UNROLL = 9
LANE = 128
SUBLANE = 8
VMEM_LIMIT = 56 * 1024 * 1024


def _params(n_axes, vmem=VMEM_LIMIT):
    return pltpu.CompilerParams(dimension_semantics=("arbitrary",) * n_axes,
                                vmem_limit_bytes=vmem)


def _tile(n, prefs):
    for t in prefs:
        if n % t == 0:
            return t
    return n


def _silu(x):
    return x / (1.0 + jnp.exp(-x))


def _mod_kernel(a_ref, w_ref, b_ref, o_ref):
    a = _silu(a_ref[...]).astype(BF16)
    o_ref[...] = jnp.dot(a, w_ref[...].astype(BF16), preferred_element_type=F32) + b_ref[...]


def _modulation(cc, w_ada, b_ada):
    depth, d, n = w_ada.shape
    tn = _tile(n, (512, 256, 128))
    return pl.pallas_call(
        _mod_kernel,
        grid=(depth, n // tn),
        in_specs=[pl.BlockSpec((SUBLANE, d), lambda l, j: (0, 0)),
                  pl.BlockSpec((None, d, tn), lambda l, j: (l, 0, j)),
                  pl.BlockSpec((None, 1, tn), lambda l, j: (l, 0, j))],
        out_specs=pl.BlockSpec((None, SUBLANE, tn), lambda l, j: (l, 0, j)),
        out_shape=jax.ShapeDtypeStruct((depth, SUBLANE, n), F32),
        compiler_params=_params(2),
        name="adaln_mod",
    )(cc, w_ada, b_ada.reshape(depth, 1, n))


def _row_spec(mods, layer, part, group_of_tile):
    d = mods.shape[-1]
    return pl.BlockSpec((None, None, None, 1, d),
                        lambda i: (layer, group_of_tile(i), part, 0, 0))


def _modulate_kernel(x_ref, sh_ref, sc_ref, h_ref):
    h_ref[...] = (x_ref[...] * (1.0 + sc_ref[...]) + sh_ref[...]).astype(h_ref.dtype)


def _modulate(x_all, mods, layer, group_of, tm):
    m, d = x_all.shape
    gof = lambda i: group_of(i, tm)
    return pl.pallas_call(
        _modulate_kernel,
        grid=(m // tm,),
        in_specs=[pl.BlockSpec((tm, d), lambda i: (i, 0)),
                  _row_spec(mods, layer, 0, gof),
                  _row_spec(mods, layer, 1, gof)],
        out_specs=pl.BlockSpec((tm, d), lambda i: (i, 0)),
        out_shape=jax.ShapeDtypeStruct((m, d), BF16),
        compiler_params=_params(1),
        name="modulate",
    )(x_all, mods, mods)


def _mm_kernel(a_ref, b_ref, o_ref):
    o_ref[...] = jnp.dot(a_ref[...], b_ref[...], preferred_element_type=F32).astype(o_ref.dtype)


def _matmul(a, w, layer, out_dtype, rows, tm, tn, name):
    k = a.shape[1]
    n = w.shape[2]
    return pl.pallas_call(
        _mm_kernel,
        grid=(n // tn, rows // tm),
        in_specs=[pl.BlockSpec((tm, k), lambda j, i: (i, 0)),
                  pl.BlockSpec((None, k, tn), lambda j, i: (layer, 0, j))],
        out_specs=pl.BlockSpec((tm, tn), lambda j, i: (i, j)),
        out_shape=jax.ShapeDtypeStruct((rows, n), out_dtype),
        compiler_params=_params(2),
        name=name,
    )(a, w)


def _mm2_kernel(a1_ref, a2_ref, b1_ref, b2_ref, o_ref):
    acc = jnp.dot(a1_ref[...], b1_ref[...], preferred_element_type=F32)
    acc += jnp.dot(a2_ref[...], b2_ref[...], preferred_element_type=F32)
    o_ref[...] = acc.astype(o_ref.dtype)


def _matmul2(a1, a2, w1, w2, layer, rows, tm, tn, name):
    k = a1.shape[1]
    n = w1.shape[2]
    return pl.pallas_call(
        _mm2_kernel,
        grid=(n // tn, rows // tm),
        in_specs=[pl.BlockSpec((tm, k), lambda j, i: (i, 0)),
                  pl.BlockSpec((tm, k), lambda j, i: (i, 0)),
                  pl.BlockSpec((None, k, tn), lambda j, i: (layer, 0, j)),
                  pl.BlockSpec((None, k, tn), lambda j, i: (layer, 0, j))],
        out_specs=pl.BlockSpec((tm, tn), lambda j, i: (i, j)),
        out_shape=jax.ShapeDtypeStruct((rows, n), F32),
        compiler_params=_params(2),
        name=name,
    )(a1, a2, w1, w2)


def _rope_half(x, cos, sin):
    return x * cos + pltpu.roll(x, ROPE_DIM, axis=1) * sin


def _rms(x, g):
    return (x * lax.rsqrt(jnp.mean(x * x, axis=-1, keepdims=True) + EPS)) * g


def _dot_nt(a, b):
    return lax.dot_general(a, b, (((1,), (1,)), ((), ())), preferred_element_type=F32)


def _up_kernel(p_ref, cos_ref, sin_ref, cos_t_ref, sin_t_ref, gq_ref, gkv_ref, wq_t_ref, wk_ref, wv_t_ref,
               q_t_ref, k_ref, v_t_ref, *, ql, kvl, heads):
    cq = _rms(p_ref[:, :ql], gq_ref[...]).astype(BF16)
    ckv = _rms(p_ref[:, ql:ql + kvl], gkv_ref[...]).astype(BF16)
    krp = _rope_half(p_ref[:, ql + kvl:], cos_ref[...], sin_ref[...]).astype(BF16)
    cos_t = cos_t_ref[...] * Q_PRESCALE
    sin_t = sin_t_ref[...] * Q_PRESCALE
    q_t = _dot_nt(wq_t_ref[...], cq)
    kn = jnp.dot(ckv, wk_ref[...], preferred_element_type=F32)
    for h in range(heads):
        lo = h * QK_PAD
        hi = q_t[lo + HEAD_DIM:lo + QK_PAD, :]
        q_t_ref[lo:lo + HEAD_DIM, :] = (q_t[lo:lo + HEAD_DIM, :] * Q_PRESCALE).astype(BF16)
        q_t_ref[lo + HEAD_DIM:lo + QK_PAD, :] = (
            hi * cos_t + pltpu.roll(hi, ROPE_DIM, axis=0) * sin_t).astype(BF16)
        k_ref[:, lo:lo + HEAD_DIM] = kn[:, h * HEAD_DIM:(h + 1) * HEAD_DIM].astype(BF16)
        k_ref[:, lo + HEAD_DIM:lo + QK_PAD] = krp
    v_t_ref[...] = _dot_nt(wv_t_ref[...], ckv).astype(BF16)


def _up_project(p_lora, tables, gq, gkv, wq_t, wk, wv_t, layer, heads, tm):
    m, pw = p_lora.shape
    ql, kvl = gq.shape[-1], gkv.shape[-1]
    depth = wq_t.shape[0]
    cos_r, sin_r, cos_t, sin_t = tables
    const = lambda i: (layer, 0, 0)
    return pl.pallas_call(
        functools.partial(_up_kernel, ql=ql, kvl=kvl, heads=heads),
        grid=(m // tm,),
        in_specs=[pl.BlockSpec((tm, pw), lambda i: (i, 0)),
                  pl.BlockSpec((tm, LANE), lambda i: (i, 0)),
                  pl.BlockSpec((tm, LANE), lambda i: (i, 0)),
                  pl.BlockSpec((LANE, tm), lambda i: (0, i)),
                  pl.BlockSpec((LANE, tm), lambda i: (0, i)),
                  pl.BlockSpec((None, 1, ql), const),
                  pl.BlockSpec((None, 1, kvl), const),
                  pl.BlockSpec((None, heads * QK_PAD, ql), const),
                  pl.BlockSpec((None, kvl, heads * HEAD_DIM), const),
                  pl.BlockSpec((None, heads * HEAD_DIM, kvl), const)],
        out_specs=[pl.BlockSpec((heads * QK_PAD, tm), lambda i: (0, i)),
                   pl.BlockSpec((tm, heads * QK_PAD), lambda i: (i, 0)),
                   pl.BlockSpec((None, heads * HEAD_DIM, tm), lambda i: (i, 0, 0))],
        out_shape=[jax.ShapeDtypeStruct((heads * QK_PAD, m), BF16),
                   jax.ShapeDtypeStruct((m, heads * QK_PAD), BF16),
                   jax.ShapeDtypeStruct((m // tm, heads * HEAD_DIM, tm), BF16)],
        compiler_params=_params(1),
        name="lora_up",
    )(p_lora, cos_r, sin_r, cos_t, sin_t, gq.reshape(depth, 1, ql), gkv.reshape(depth, 1, kvl), wq_t, wk, wv_t)


def _attn_kernel(q_t_ref, kl_ref, kc_ref, vl_ref, vc_ref, o_ref,
                 s0_ref, s1_ref, s2_ref, p0_ref, p1_ref, p2_ref, a0_ref, a1_ref, a2_ref,
                 c0_ref, c1_ref, c2_ref, m_ref, l_ref, acc_ref, *, tk, unroll):
    n_lat = kl_ref.shape[0] // tk
    n = n_lat + 1
    s_buf, p_buf, a_buf = (s0_ref, s1_ref, s2_ref), (p0_ref, p1_ref, p2_ref), (a0_ref, a1_ref, a2_ref)
    c_buf = (c0_ref, c1_ref, c2_ref)
    q_t = q_t_ref[...]

    def k_chunk(c):
        if isinstance(c, int) and c == n_lat:
            return kc_ref[...]
        return kl_ref[pl.ds(pl.multiple_of(c * tk, tk), tk), :]

    def v_chunk(c):
        if isinstance(c, int) and c == n_lat:
            return vc_ref[0]
        return vl_ref[c]

    def scores(c, slot):
        s = jnp.dot(k_chunk(c), q_t, preferred_element_type=F32)
        s_buf[slot][...] = s
        c_buf[slot][...] = jnp.max(s, axis=0, keepdims=True)

    def softmax(slot):
        m_old = m_ref[...]
        m_new = jnp.maximum(m_old, c_buf[slot][...])
        alpha = jnp.exp2(m_old - m_new)
        p = jnp.exp2(s_buf[slot][...] - m_new)
        l_ref[...] = alpha * l_ref[...] + jnp.sum(p, axis=0, keepdims=True)
        m_ref[...] = m_new
        a_buf[slot][...] = alpha
        p_buf[slot][...] = p.astype(BF16)

    def values(c, slot):
        acc_ref[...] = a_buf[slot][...] * acc_ref[...] + jnp.dot(
            v_chunk(c), p_buf[slot][...], preferred_element_type=F32)

    def stage(c, slot, with_scores=True, with_values=True):
        if with_scores:
            scores(c + 2, (slot + 2) % 3)
        softmax(slot)
        if with_values:
            values(c - 2, (slot + 1) % 3)

    m_ref[...] = jnp.full(m_ref.shape, -jnp.inf, F32)
    l_ref[...] = jnp.zeros(l_ref.shape, F32)
    acc_ref[...] = jnp.zeros(acc_ref.shape, F32)
    scores(0, 0)
    scores(1, 1)
    for c in range(2):
        stage(c, c, with_scores=c + 2 < n, with_values=False)
    n_trips = max(n - 5, 0) // unroll

    def trip(j, carry):
        for r in range(unroll):
            stage(2 + unroll * j + r, (2 + r) % 3)
        return carry

    lax.fori_loop(0, n_trips, trip, 0)
    for c in range(2 + unroll * n_trips, n):
        stage(c, c % 3, with_scores=c + 2 < n)
    values(n - 2, (n - 2) % 3)
    values(n - 1, (n - 1) % 3)
    o_t = acc_ref[...] / l_ref[...]
    o_ref[...] = o_t.T.astype(o_ref.dtype)


def _attend_latent(q_t, k, v_t, batch, seq, ctx_len, heads, tq, tk):
    n_q = seq // tq
    ctx0 = batch * seq // ctx_len
    n_lat = seq // tk
    return pl.pallas_call(
        functools.partial(_attn_kernel, tk=tk, unroll=ATTN_UNROLL),
        grid=(batch, heads, n_q),
        in_specs=[pl.BlockSpec((QK_PAD, tq), lambda b, h, i: (h, b * n_q + i)),
                  pl.BlockSpec((seq, QK_PAD), lambda b, h, i: (b, h)),
                  pl.BlockSpec((ctx_len, QK_PAD), lambda b, h, i: (ctx0 + b, h)),
                  pl.BlockSpec((n_lat, HEAD_DIM, tk), lambda b, h, i: (b, h, 0)),
                  pl.BlockSpec((ctx_len // tk, HEAD_DIM, tk), lambda b, h, i: (ctx0 + b, h, 0))],
        out_specs=pl.BlockSpec((tq, HEAD_DIM), lambda b, h, i: (b * n_q + i, h)),
        out_shape=jax.ShapeDtypeStruct((batch * seq, heads * HEAD_DIM), BF16),
        scratch_shapes=[pltpu.VMEM((tk, tq), F32)] * 3 + [pltpu.VMEM((tk, tq), BF16)] * 3 + [
                        pltpu.VMEM((1, tq), F32)] * 6 + [
                        pltpu.VMEM((1, tq), F32), pltpu.VMEM((1, tq), F32),
                        pltpu.VMEM((HEAD_DIM, tq), F32)],
        compiler_params=_params(3),
        name="attn_latent",
    )(q_t, k, k, v_t, v_t)


def _attn_ctx_kernel(q_t_ref, k_ref, v_t_ref, o_ref):
    s = jnp.dot(k_ref[...], q_t_ref[...], preferred_element_type=F32)
    p = jnp.exp2(s - jnp.max(s, axis=0, keepdims=True))
    o_t = jnp.dot(v_t_ref[0], p.astype(BF16), preferred_element_type=F32)
    o_ref[...] = (o_t / jnp.sum(p, axis=0, keepdims=True)).T.astype(o_ref.dtype)


def _attend_ctx(q_t, k, v_t, batch, seq, ctx_len, heads):
    ctx0 = batch * seq // ctx_len
    return pl.pallas_call(
        _attn_ctx_kernel,
        grid=(batch, heads),
        in_specs=[pl.BlockSpec((QK_PAD, ctx_len), lambda b, h: (h, ctx0 + b)),
                  pl.BlockSpec((ctx_len, QK_PAD), lambda b, h: (ctx0 + b, h)),
                  pl.BlockSpec((1, HEAD_DIM, ctx_len), lambda b, h: (ctx0 + b, h, 0))],
        out_specs=pl.BlockSpec((ctx_len, HEAD_DIM), lambda b, h: (b, h)),
        out_shape=jax.ShapeDtypeStruct((batch * ctx_len, heads * HEAD_DIM), BF16),
        compiler_params=_params(2),
        name="attn_ctx",
    )(q_t, k, v_t)


def _merge_kernel(u_ref, bg_ref, cg_ref, zc_ref, zm_ref, up_ref, cp_ref, un_ref, cn_ref,
                  ya_ref, cw_ref, yc_ref, ym_ref, *, tm, lat_rows, seq, ctx_len):
    i = pl.program_id(0)
    row0 = i * tm
    row1 = row0 + tm
    is_start = jnp.where(row0 < lat_rows, row0 % seq == 0, (row0 - lat_rows) % ctx_len == 0)
    is_end = jnp.where(row1 <= lat_rows, row1 % seq == 0, (row1 - lat_rows) % ctx_len == 0)
    v = cg_ref[...].astype(F32) * u_ref[...].astype(F32)
    v_before = (cp_ref[SUBLANE - 1:SUBLANE, :].astype(F32) * up_ref[SUBLANE - 1:SUBLANE, :].astype(F32))
    v_after = cn_ref[0:1, :].astype(F32) * un_ref[0:1, :].astype(F32)
    v_before = jnp.where(is_start, 0.0, v_before)
    v_after = jnp.where(is_end, 0.0, v_after)
    row = lax.broadcasted_iota(jnp.int32, v.shape, 0)
    v_prev = jnp.where(row == 0, v_before, pltpu.roll(v, 1, axis=0))
    v_next = jnp.where(row == tm - 1, v_after, pltpu.roll(v, tm - 1, axis=0))
    cw = cw_ref[...]
    y = v_prev * cw[0:1, :] + v * cw[1:2, :] + v_next * cw[2:3, :]
    yc_ref[...] = (bg_ref[...].astype(F32) * y * _silu(zc_ref[...].astype(F32))).astype(yc_ref.dtype)
    ym_ref[...] = (ya_ref[...].astype(F32) * _silu(zm_ref[...].astype(F32))).astype(ym_ref.dtype)


def _merge(p_mix, y_attn, conv_w, layer, rows, lat_rows, seq, ctx_len, tm, tc):
    m_all = p_mix.shape[0]
    cw = conv_w.shape[-1]
    nc = cw // tc
    hb = tm // SUBLANE
    last_hb = m_all // SUBLANE - 1
    main = lambda s: pl.BlockSpec((tm, tc), lambda i, j: (i, s * nc + j))
    prev = lambda s: pl.BlockSpec((SUBLANE, tc), lambda i, j: (jnp.maximum(i * hb - 1, 0), s * nc + j))
    nxt = lambda s: pl.BlockSpec((SUBLANE, tc), lambda i, j: (jnp.minimum((i + 1) * hb, last_hb), s * nc + j))
    out = pl.BlockSpec((tm, tc), lambda i, j: (i, j))
    return pl.pallas_call(
        functools.partial(_merge_kernel, tm=tm, lat_rows=lat_rows, seq=seq, ctx_len=ctx_len),
        grid=(rows // tm, nc),
        in_specs=[main(0), main(1), main(2), main(3), main(4),
                  prev(0), prev(2), nxt(0), nxt(2),
                  pl.BlockSpec((tm, tc), lambda i, j: (i, j)),
                  pl.BlockSpec((None, 3, tc), lambda i, j: (layer, 0, j))],
        out_specs=[out, out],
        out_shape=[jax.ShapeDtypeStruct((rows, cw), BF16), jax.ShapeDtypeStruct((rows, cw), BF16)],
        compiler_params=_params(2),
        name="conv_merge",
    )(p_mix, p_mix, p_mix, p_mix, p_mix, p_mix, p_mix, p_mix, p_mix, y_attn, conv_w)


def _layernorm(z, g, b):
    mu = jnp.mean(z, axis=-1, keepdims=True)
    zc = z - mu
    var = jnp.mean(zc * zc, axis=-1, keepdims=True)
    return zc * lax.rsqrt(var + EPS) * g + b


def _postln_next_kernel(x_ref, o_ref, gate_ref, g_ref, b_ref, sh_ref, sc_ref, xn_ref, h_ref, *, alpha):
    xn = _layernorm(alpha * x_ref[...] + gate_ref[...] * o_ref[...], g_ref[...], b_ref[...])
    xn_ref[...] = xn
    h_ref[...] = (xn * (1.0 + sc_ref[...]) + sh_ref[...]).astype(h_ref.dtype)


def _postln_last_kernel(x_ref, o_ref, gate_ref, g_ref, b_ref, xn_ref, *, alpha):
    xn_ref[...] = _layernorm(alpha * x_ref[...] + gate_ref[...] * o_ref[...], g_ref[...], b_ref[...])


def _postln(x_all, out, mods, ln_g, ln_b, layer, rows, group_of, tm, alpha, last):
    d = x_all.shape[1]
    depth = ln_g.shape[0]
    gof = lambda i: group_of(i, tm)
    tile = pl.BlockSpec((tm, d), lambda i: (i, 0))
    vec = pl.BlockSpec((None, 1, d), lambda i: (layer, 0, 0))
    in_specs = [tile, tile, _row_spec(mods, layer, 2, gof), vec, vec]
    args = [x_all, out, mods, ln_g.reshape(depth, 1, d), ln_b.reshape(depth, 1, d)]
    if last:
        return pl.pallas_call(
            functools.partial(_postln_last_kernel, alpha=alpha),
            grid=(rows // tm,), in_specs=in_specs, out_specs=tile,
            out_shape=jax.ShapeDtypeStruct((rows, d), F32),
            compiler_params=_params(1), name="postln_last",
        )(*args)
    in_specs += [_row_spec(mods, layer + 1, 0, gof), _row_spec(mods, layer + 1, 1, gof)]
    args += [mods, mods]
    return pl.pallas_call(
        functools.partial(_postln_next_kernel, alpha=alpha),
        grid=(rows // tm,), in_specs=in_specs, out_specs=[tile, tile],
        out_shape=[jax.ShapeDtypeStruct((rows, d), F32), jax.ShapeDtypeStruct((rows, d), BF16)],
        compiler_params=_params(1), name="postln_next",
    )(*args)


def _rot_cols(w):
    shp = w.shape
    wr = w.reshape(shp[:-1] + (2, 2, ROPE_FREQ))
    return jnp.stack([-wr[..., 1, :], wr[..., 0, :]], axis=-2).reshape(shp)


def _rope_tables(batch, seq, ctx_rows):
    rows = seq // GRID_W
    row = jnp.repeat(jnp.arange(rows, dtype=F32), GRID_W)
    col = jnp.tile(jnp.arange(GRID_W, dtype=F32), rows)
    pos = jnp.stack([row, col], axis=-1)
    inv_freq = ROPE_BASE ** (-(jnp.arange(ROPE_FREQ, dtype=F32) * 2.0) / (ROPE_DIM // 2))
    ang = pos[:, :, None] * inv_freq
    expand = lambda t: jnp.broadcast_to(t[:, :, None, :], (seq, 2, 2, ROPE_FREQ)).reshape(seq, ROPE_DIM)
    cos = jnp.concatenate([jnp.tile(expand(jnp.cos(ang)), (batch, 1)), jnp.ones((ctx_rows, ROPE_DIM), F32)])
    sin = jnp.concatenate([jnp.tile(expand(jnp.sin(ang)), (batch, 1)), jnp.zeros((ctx_rows, ROPE_DIM), F32)])
    pad = lambda t: jnp.pad(t, ((0, 0), (0, LANE - ROPE_DIM)))
    cos, sin = pad(cos), pad(sin)
    return cos, sin, cos.T, sin.T


def kernel(x, c, ctx, c_ctx, w_ada, b_ada, w_in, conv_w, q_norm_g, w_uq, kv_norm_g, w_ukv, w_out, ln_g, ln_b):
    batch, seq, d = x.shape
    ctx_len = ctx.shape[1]
    depth = w_ada.shape[0]
    cw = conv_w.shape[-1]
    heads = (w_out.shape[1] - cw) // HEAD_DIM
    ql, kvl = q_norm_g.shape[-1], kv_norm_g.shape[-1]
    lat_rows, ctx_rows = batch * seq, batch * ctx_len
    m_all = lat_rows + ctx_rows
    alpha = (2.0 * depth) ** 0.25

    sec = [cw, cw, cw, cw, ql, kvl, ROPE_DIM, cw]
    offs = [0]
    for s in sec:
        offs.append(offs[-1] + s)
    col = lambda k: w_in[:, :, offs[k]:offs[k + 1]]
    w_mix = jnp.concatenate([col(0), col(1), col(2), col(3), col(7)], axis=-1).astype(BF16)
    w_lora = jnp.concatenate([col(4), col(5), col(6), _rot_cols(col(6))], axis=-1).astype(BF16)
    wq = w_uq.reshape(depth, ql, heads, HEAD_DIM + ROPE_DIM)
    wq = jnp.concatenate([wq, _rot_cols(wq[..., HEAD_DIM:])], axis=-1).reshape(depth, ql, heads * QK_PAD)
    wq_t = jnp.swapaxes(wq, 1, 2).astype(BF16)
    wkv = w_ukv.reshape(depth, kvl, heads, 2 * HEAD_DIM)
    wk = wkv[..., :HEAD_DIM].reshape(depth, kvl, heads * HEAD_DIM).astype(BF16)
    wv_t = jnp.swapaxes(wkv[..., HEAD_DIM:].reshape(depth, kvl, heads * HEAD_DIM), 1, 2).astype(BF16)
    wo_conv = w_out[:, :cw, :].astype(BF16)
    wo_mla = w_out[:, cw:, :].astype(BF16)
    tables = _rope_tables(batch, seq, ctx_rows)

    cc = jnp.zeros((SUBLANE, d), F32).at[:batch].set(c).at[batch].set(c_ctx)
    mods = _modulation(cc, w_ada, b_ada)[:, :batch + 1].reshape(depth, batch + 1, 3, 1, d)
    group_of = lambda i, tm: jnp.minimum((i * tm) // seq, batch)

    tm = _tile(math.gcd(seq, ctx_rows), (512, 256, 128))
    tm_small = _tile(math.gcd(seq, ctx_len), (256, 128))
    assert tm_small == ctx_len, "the context keys must form exactly one key chunk"
    x_all = jnp.concatenate([x.reshape(lat_rows, d), ctx.reshape(ctx_rows, d)], axis=0)
    h = _modulate(x_all, mods, 0, group_of, tm)

    for layer in range(depth):
        last = layer == depth - 1
        rows = lat_rows if last else m_all
        p_mix = _matmul(h, w_mix, layer, BF16, rows, tm, _tile(5 * cw, (1024, 512, 256, 128)), "in_proj_mix")
        p_lora = _matmul(h, w_lora, layer, F32, m_all, tm, w_lora.shape[-1], "in_proj_lora")
        q_t, k, v_t = _up_project(p_lora, tables, q_norm_g, kv_norm_g, wq_t, wk, wv_t, layer, heads, tm_small)
        y_attn = _attend_latent(q_t, k, v_t, batch, seq, ctx_len, heads, _tile(seq, (1024, 512, 256, 128)), tm_small)
        if not last:
            y_attn = jnp.concatenate([y_attn, _attend_ctx(q_t, k, v_t, batch, seq, ctx_len, heads)], axis=0)
        y_conv, y_mla = _merge(p_mix, y_attn, conv_w, layer, rows, lat_rows, seq, ctx_len,
                               tm_small, _tile(cw, (512, 256, 128)))
        out = _matmul2(y_conv, y_mla, wo_conv, wo_mla, layer, rows, tm, _tile(d, (1024, 512, 256, 128)), "out_proj")
        if last:
            x_all = _postln(x_all, out, mods, ln_g, ln_b, layer, rows, group_of, tm_small, alpha, True)
        else:
            x_all, h = _postln(x_all, out, mods, ln_g, ln_b, layer, rows, group_of, tm_small, alpha, False)
    return x_all.reshape(batch, seq, d)
```

```python
import functools
import math

import jax
import jax.numpy as jnp
from jax import lax
from jax.experimental import pallas as pl
from jax.experimental.pallas import tpu as pltpu

F32 = jnp.float32
BF16 = jnp.bfloat16

HEAD_DIM = 128
ROPE_DIM = 64
ROPE_FREQ = ROPE_DIM // 4
ROPE_BASE = 10000.0
GRID_W = 64
QK_PAD = 2 * HEAD_DIM
V_ROWS = HEAD_DIM + 16
EPS = 1e-6
ATTN_SCALE = 1.0 / math.sqrt(HEAD_DIM + ROPE_DIM)
Q_PRESCALE = ATTN_SCALE * math.log2(math.e)
ATTN_UNROLL = 9
LANE = 128
SUBLANE = 8
VMEM_LIMIT = 56 * 1024 * 1024


def _params(n_axes, vmem=VMEM_LIMIT):
    return pltpu.CompilerParams(dimension_semantics=("arbitrary",) * n_axes,
                                vmem_limit_bytes=vmem)


def _tile(n, prefs):
    for t in prefs:
        if n % t == 0:
            return t
    return n


def _silu(x):
    return x / (1.0 + jnp.exp(-x))


def _mod_kernel(a_ref, w_ref, b_ref, o_ref):
    a = _silu(a_ref[...]).astype(BF16)
    o_ref[...] = jnp.dot(a, w_ref[...].astype(BF16), preferred_element_type=F32) + b_ref[...]


def _modulation(cc, w_ada, b_ada):
    depth, d, n = w_ada.shape
    tn = _tile(n, (512, 256, 128))
    return pl.pallas_call(
        _mod_kernel,
        grid=(depth, n // tn),
        in_specs=[pl.BlockSpec((SUBLANE, d), lambda l, j: (0, 0)),
                  pl.BlockSpec((None, d, tn), lambda l, j: (l, 0, j)),
                  pl.BlockSpec((None, 1, tn), lambda l, j: (l, 0, j))],
        out_specs=pl.BlockSpec((None, SUBLANE, tn), lambda l, j: (l, 0, j)),
        out_shape=jax.ShapeDtypeStruct((depth, SUBLANE, n), F32),
        compiler_params=_params(2),
        name="adaln_mod",
    )(cc, w_ada, b_ada.reshape(depth, 1, n))


def _row_spec(mods, layer, part, group_of_tile):
    d = mods.shape[-1]
    return pl.BlockSpec((None, None, None, 1, d),
                        lambda i: (layer, group_of_tile(i), part, 0, 0))


def _modulate_kernel(x_ref, sh_ref, sc_ref, h_ref):
    h_ref[...] = (x_ref[...] * (1.0 + sc_ref[...]) + sh_ref[...]).astype(h_ref.dtype)


def _modulate(x_all, mods, layer, group_of, tm):
    m, d = x_all.shape
    gof = lambda i: group_of(i, tm)
    return pl.pallas_call(
        _modulate_kernel,
        grid=(m // tm,),
        in_specs=[pl.BlockSpec((tm, d), lambda i: (i, 0)),
                  _row_spec(mods, layer, 0, gof),
                  _row_spec(mods, layer, 1, gof)],
        out_specs=pl.BlockSpec((tm, d), lambda i: (i, 0)),
        out_shape=jax.ShapeDtypeStruct((m, d), BF16),
        compiler_params=_params(1),
        name="modulate",
    )(x_all, mods, mods)


def _mm_kernel(a_ref, b_ref, o_ref):
    o_ref[...] = jnp.dot(a_ref[...], b_ref[...], preferred_element_type=F32).astype(o_ref.dtype)


def _matmul(a, w, layer, out_dtype, rows, tm, tn, name):
    k = a.shape[1]
    n = w.shape[2]
    return pl.pallas_call(
        _mm_kernel,
        grid=(n // tn, rows // tm),
        in_specs=[pl.BlockSpec((tm, k), lambda j, i: (i, 0)),
                  pl.BlockSpec((None, k, tn), lambda j, i: (layer, 0, j))],
        out_specs=pl.BlockSpec((tm, tn), lambda j, i: (i, j)),
        out_shape=jax.ShapeDtypeStruct((rows, n), out_dtype),
        compiler_params=_params(2),
        name=name,
    )(a, w)


def _mm2_kernel(a1_ref, a2_ref, b1_ref, b2_ref, o_ref):
    acc = jnp.dot(a1_ref[...], b1_ref[...], preferred_element_type=F32)
    acc += jnp.dot(a2_ref[...], b2_ref[...], preferred_element_type=F32)
    o_ref[...] = acc.astype(o_ref.dtype)


def _matmul2(a1, a2, w1, w2, layer, rows, tm, tn, name):
    k = a1.shape[1]
    n = w1.shape[2]
    return pl.pallas_call(
        _mm2_kernel,
        grid=(n // tn, rows // tm),
        in_specs=[pl.BlockSpec((tm, k), lambda j, i: (i, 0)),
                  pl.BlockSpec((tm, k), lambda j, i: (i, 0)),
                  pl.BlockSpec((None, k, tn), lambda j, i: (layer, 0, j)),
                  pl.BlockSpec((None, k, tn), lambda j, i: (layer, 0, j))],
        out_specs=pl.BlockSpec((tm, tn), lambda j, i: (i, j)),
        out_shape=jax.ShapeDtypeStruct((rows, n), F32),
        compiler_params=_params(2),
        name=name,
    )(a1, a2, w1, w2)


def _rope_half(x, cos, sin):
    return x * cos + pltpu.roll(x, ROPE_DIM, axis=1) * sin


def _rms(x, g):
    return (x * lax.rsqrt(jnp.mean(x * x, axis=-1, keepdims=True) + EPS)) * g


def _dot_nt(a, b):
    return lax.dot_general(a, b, (((1,), (1,)), ((), ())), preferred_element_type=F32)


def _up_kernel(p_ref, cos_ref, sin_ref, cos_t_ref, sin_t_ref, gq_ref, gkv_ref, wq_t_ref, wk_ref, wv_t_ref,
               q_t_ref, k_ref, v_t_ref, *, ql, kvl, heads):
    cq = _rms(p_ref[:, :ql], gq_ref[...]).astype(BF16)
    ckv = _rms(p_ref[:, ql:ql + kvl], gkv_ref[...]).astype(BF16)
    krp = _rope_half(p_ref[:, ql + kvl:], cos_ref[...], sin_ref[...]).astype(BF16)
    cos_t = cos_t_ref[...] * Q_PRESCALE
    sin_t = sin_t_ref[...] * Q_PRESCALE
    q_t = _dot_nt(wq_t_ref[...], cq)
    kn = jnp.dot(ckv, wk_ref[...], preferred_element_type=F32)
    for h in range(heads):
        lo = h * QK_PAD
        hi = q_t[lo + HEAD_DIM:lo + QK_PAD, :]
        q_t_ref[lo:lo + HEAD_DIM, :] = (q_t[lo:lo + HEAD_DIM, :] * Q_PRESCALE).astype(BF16)
        q_t_ref[lo + HEAD_DIM:lo + QK_PAD, :] = (
            hi * cos_t + pltpu.roll(hi, ROPE_DIM, axis=0) * sin_t).astype(BF16)
        k_ref[:, lo:lo + HEAD_DIM] = kn[:, h * HEAD_DIM:(h + 1) * HEAD_DIM].astype(BF16)
        k_ref[:, lo + HEAD_DIM:lo + QK_PAD] = krp
    v_t = _dot_nt(wv_t_ref[...], ckv)
    for h in range(heads):
        lo = h * V_ROWS
        v_t_ref[lo:lo + HEAD_DIM, :] = v_t[h * HEAD_DIM:(h + 1) * HEAD_DIM, :].astype(BF16)
        v_t_ref[lo + HEAD_DIM:lo + V_ROWS, :] = jnp.ones((V_ROWS - HEAD_DIM, v_t.shape[1]), BF16)


def _up_project(p_lora, tables, gq, gkv, wq_t, wk, wv_t, layer, heads, tm):
    m, pw = p_lora.shape
    ql, kvl = gq.shape[-1], gkv.shape[-1]
    depth = wq_t.shape[0]
    cos_r, sin_r, cos_t, sin_t = tables
    const = lambda i: (layer, 0, 0)
    return pl.pallas_call(
        functools.partial(_up_kernel, ql=ql, kvl=kvl, heads=heads),
        grid=(m // tm,),
        in_specs=[pl.BlockSpec((tm, pw), lambda i: (i, 0)),
                  pl.BlockSpec((tm, LANE), lambda i: (i, 0)),
                  pl.BlockSpec((tm, LANE), lambda i: (i, 0)),
                  pl.BlockSpec((LANE, tm), lambda i: (0, i)),
                  pl.BlockSpec((LANE, tm), lambda i: (0, i)),
                  pl.BlockSpec((None, 1, ql), const),
                  pl.BlockSpec((None, 1, kvl), const),
                  pl.BlockSpec((None, heads * QK_PAD, ql), const),
                  pl.BlockSpec((None, kvl, heads * HEAD_DIM), const),
                  pl.BlockSpec((None, heads * HEAD_DIM, kvl), const)],
        out_specs=[pl.BlockSpec((heads * QK_PAD, tm), lambda i: (0, i)),
                   pl.BlockSpec((tm, heads * QK_PAD), lambda i: (i, 0)),
                   pl.BlockSpec((None, heads * V_ROWS, tm), lambda i: (i, 0, 0))],
        out_shape=[jax.ShapeDtypeStruct((heads * QK_PAD, m), BF16),
                   jax.ShapeDtypeStruct((m, heads * QK_PAD), BF16),
                   jax.ShapeDtypeStruct((m // tm, heads * V_ROWS, tm), BF16)],
        compiler_params=_params(1),
        name="lora_up",
    )(p_lora, cos_r, sin_r, cos_t, sin_t, gq.reshape(depth, 1, ql), gkv.reshape(depth, 1, kvl), wq_t, wk, wv_t)


def _attn_kernel(q_t_ref, kl_ref, kc_ref, vl_ref, vc_ref, o_ref,
                 s0_ref, s1_ref, s2_ref, p0_ref, p1_ref, p2_ref, a0_ref, a1_ref, a2_ref,
                 c0_ref, c1_ref, c2_ref, m_ref, acc_ref, *, tk, unroll):
    n_lat = kl_ref.shape[0] // tk
    n = n_lat + 1
    s_buf, p_buf, a_buf = (s0_ref, s1_ref, s2_ref), (p0_ref, p1_ref, p2_ref), (a0_ref, a1_ref, a2_ref)
    c_buf = (c0_ref, c1_ref, c2_ref)
    q_t = q_t_ref[...]

    def k_chunk(c):
        if isinstance(c, int) and c == n_lat:
            return kc_ref[...]
        return kl_ref[pl.ds(pl.multiple_of(c * tk, tk), tk), :]

    def v_chunk(c):
        if isinstance(c, int) and c == n_lat:
            return vc_ref[0]
        return vl_ref[c]

    def scores(c, slot):
        s = jnp.dot(k_chunk(c), q_t, preferred_element_type=F32)
        s_buf[slot][...] = s
        c_buf[slot][...] = jnp.max(s, axis=0, keepdims=True)

    def softmax(slot):
        m_old = m_ref[...]
        m_new = jnp.maximum(m_old, c_buf[slot][...])
        alpha = jnp.exp2(m_old - m_new)
        m_ref[...] = m_new
        a_buf[slot][...] = alpha
        p_buf[slot][...] = jnp.exp2((s_buf[slot][...] - m_new).astype(BF16))

    def values(c, slot):
        acc_ref[...] = a_buf[slot][...] * acc_ref[...] + jnp.dot(
            v_chunk(c), p_buf[slot][...], preferred_element_type=F32)

    def stage(c, slot, with_scores=True, with_values=True):
        if with_scores:
            scores(c + 2, (slot + 2) % 3)
        softmax(slot)
        if with_values:
            values(c - 2, (slot + 1) % 3)

    m_ref[...] = jnp.full(m_ref.shape, -jnp.inf, F32)
    acc_ref[...] = jnp.zeros(acc_ref.shape, F32)
    scores(0, 0)
    scores(1, 1)
    for c in range(2):
        stage(c, c, with_scores=c + 2 < n, with_values=False)
    n_trips = max(n - 5, 0) // unroll

    def trip(j, carry):
        for r in range(unroll):
            stage(2 + unroll * j + r, (2 + r) % 3)
        return carry

    lax.fori_loop(0, n_trips, trip, 0)
    for c in range(2 + unroll * n_trips, n):
        stage(c, c % 3, with_scores=c + 2 < n)
    values(n - 2, (n - 2) % 3)
    values(n - 1, (n - 1) % 3)
    o_t = acc_ref[:HEAD_DIM, :] / acc_ref[HEAD_DIM:HEAD_DIM + 1, :]
    o_ref[...] = o_t.T.astype(o_ref.dtype)


def _attend_latent(q_t, k, v_t, batch, seq, ctx_len, heads, tq, tk):
    n_q = seq // tq
    ctx0 = batch * seq // ctx_len
    n_lat = seq // tk
    return pl.pallas_call(
        functools.partial(_attn_kernel, tk=tk, unroll=ATTN_UNROLL),
        grid=(batch, heads, n_q),
        in_specs=[pl.BlockSpec((QK_PAD, tq), lambda b, h, i: (h, b * n_q + i)),
                  pl.BlockSpec((seq, QK_PAD), lambda b, h, i: (b, h)),
                  pl.BlockSpec((ctx_len, QK_PAD), lambda b, h, i: (ctx0 + b, h)),
                  pl.BlockSpec((n_lat, V_ROWS, tk), lambda b, h, i: (b, h, 0)),
                  pl.BlockSpec((ctx_len // tk, V_ROWS, tk), lambda b, h, i: (ctx0 + b, h, 0))],
        out_specs=pl.BlockSpec((tq, HEAD_DIM), lambda b, h, i: (b * n_q + i, h)),
        out_shape=jax.ShapeDtypeStruct((batch * seq, heads * HEAD_DIM), BF16),
        scratch_shapes=[pltpu.VMEM((tk, tq), F32)] * 3 + [pltpu.VMEM((tk, tq), BF16)] * 3 + [
                        pltpu.VMEM((1, tq), F32)] * 6 + [
                        pltpu.VMEM((1, tq), F32), pltpu.VMEM((V_ROWS, tq), F32)],
        compiler_params=_params(3),
        name="attn_latent",
    )(q_t, k, k, v_t, v_t)


def _attn_ctx_kernel(q_t_ref, k_ref, v_t_ref, o_ref):
    s = jnp.dot(k_ref[...], q_t_ref[...], preferred_element_type=F32)
    p = jnp.exp2((s - jnp.max(s, axis=0, keepdims=True)).astype(BF16))
    o_t = jnp.dot(v_t_ref[0], p, preferred_element_type=F32)
    o_ref[...] = (o_t[:HEAD_DIM, :] / o_t[HEAD_DIM:HEAD_DIM + 1, :]).T.astype(o_ref.dtype)


def _attend_ctx(q_t, k, v_t, batch, seq, ctx_len, heads):
    ctx0 = batch * seq // ctx_len
    return pl.pallas_call(
        _attn_ctx_kernel,
        grid=(batch, heads),
        in_specs=[pl.BlockSpec((QK_PAD, ctx_len), lambda b, h: (h, ctx0 + b)),
                  pl.BlockSpec((ctx_len, QK_PAD), lambda b, h: (ctx0 + b, h)),
                  pl.BlockSpec((1, V_ROWS, ctx_len), lambda b, h: (ctx0 + b, h, 0))],
        out_specs=pl.BlockSpec((ctx_len, HEAD_DIM), lambda b, h: (b, h)),
        out_shape=jax.ShapeDtypeStruct((batch * ctx_len, heads * HEAD_DIM), BF16),
        compiler_params=_params(2),
        name="attn_ctx",
    )(q_t, k, v_t)


def _merge_kernel(u_ref, bg_ref, cg_ref, zc_ref, zm_ref, up_ref, cp_ref, un_ref, cn_ref,
                  ya_ref, cw_ref, yc_ref, ym_ref, *, tm, lat_rows, seq, ctx_len):
    i = pl.program_id(0)
    row0 = i * tm
    row1 = row0 + tm
    is_start = jnp.where(row0 < lat_rows, row0 % seq == 0, (row0 - lat_rows) % ctx_len == 0)
    is_end = jnp.where(row1 <= lat_rows, row1 % seq == 0, (row1 - lat_rows) % ctx_len == 0)
    v = cg_ref[...].astype(F32) * u_ref[...].astype(F32)
    v_before = (cp_ref[SUBLANE - 1:SUBLANE, :].astype(F32) * up_ref[SUBLANE - 1:SUBLANE, :].astype(F32))
    v_after = cn_ref[0:1, :].astype(F32) * un_ref[0:1, :].astype(F32)
    v_before = jnp.where(is_start, 0.0, v_before)
    v_after = jnp.where(is_end, 0.0, v_after)
    row = lax.broadcasted_iota(jnp.int32, v.shape, 0)
    v_prev = jnp.where(row == 0, v_before, pltpu.roll(v, 1, axis=0))
    v_next = jnp.where(row == tm - 1, v_after, pltpu.roll(v, tm - 1, axis=0))
    cw = cw_ref[...]
    y = v_prev * cw[0:1, :] + v * cw[1:2, :] + v_next * cw[2:3, :]
    yc_ref[...] = (bg_ref[...].astype(F32) * y * _silu(zc_ref[...].astype(F32))).astype(yc_ref.dtype)
    ym_ref[...] = (ya_ref[...].astype(F32) * _silu(zm_ref[...].astype(F32))).astype(ym_ref.dtype)


def _merge(p_mix, y_attn, conv_w, layer, rows, lat_rows, seq, ctx_len, tm, tc):
    m_all = p_mix.shape[0]
    cw = conv_w.shape[-1]
    nc = cw // tc
    hb = tm // SUBLANE
    last_hb = m_all // SUBLANE - 1
    main = lambda s: pl.BlockSpec((tm, tc), lambda i, j: (i, s * nc + j))
    prev = lambda s: pl.BlockSpec((SUBLANE, tc), lambda i, j: (jnp.maximum(i * hb - 1, 0), s * nc + j))
    nxt = lambda s: pl.BlockSpec((SUBLANE, tc), lambda i, j: (jnp.minimum((i + 1) * hb, last_hb), s * nc + j))
    out = pl.BlockSpec((tm, tc), lambda i, j: (i, j))
    return pl.pallas_call(
        functools.partial(_merge_kernel, tm=tm, lat_rows=lat_rows, seq=seq, ctx_len=ctx_len),
        grid=(rows // tm, nc),
        in_specs=[main(0), main(1), main(2), main(3), main(4),
                  prev(0), prev(2), nxt(0), nxt(2),
                  pl.BlockSpec((tm, tc), lambda i, j: (i, j)),
                  pl.BlockSpec((None, 3, tc), lambda i, j: (layer, 0, j))],
        out_specs=[out, out],
        out_shape=[jax.ShapeDtypeStruct((rows, cw), BF16), jax.ShapeDtypeStruct((rows, cw), BF16)],
        compiler_params=_params(2),
        name="conv_merge",
    )(p_mix, p_mix, p_mix, p_mix, p_mix, p_mix, p_mix, p_mix, p_mix, y_attn, conv_w)


def _layernorm(z, g, b):
    mu = jnp.mean(z, axis=-1, keepdims=True)
    zc = z - mu
    var = jnp.mean(zc * zc, axis=-1, keepdims=True)
    return zc * lax.rsqrt(var + EPS) * g + b


def _postln_next_kernel(x_ref, o_ref, gate_ref, g_ref, b_ref, sh_ref, sc_ref, xn_ref, h_ref, *, alpha):
    xn = _layernorm(alpha * x_ref[...] + gate_ref[...] * o_ref[...], g_ref[...], b_ref[...])
    xn_ref[...] = xn
    h_ref[...] = (xn * (1.0 + sc_ref[...]) + sh_ref[...]).astype(h_ref.dtype)


def _postln_last_kernel(x_ref, o_ref, gate_ref, g_ref, b_ref, xn_ref, *, alpha):
    xn_ref[...] = _layernorm(alpha * x_ref[...] + gate_ref[...] * o_ref[...], g_ref[...], b_ref[...])


def _postln(x_all, out, mods, ln_g, ln_b, layer, rows, group_of, tm, alpha, last):
    d = x_all.shape[1]
    depth = ln_g.shape[0]
    gof = lambda i: group_of(i, tm)
    tile = pl.BlockSpec((tm, d), lambda i: (i, 0))
    vec = pl.BlockSpec((None, 1, d), lambda i: (layer, 0, 0))
    in_specs = [tile, tile, _row_spec(mods, layer, 2, gof), vec, vec]
    args = [x_all, out, mods, ln_g.reshape(depth, 1, d), ln_b.reshape(depth, 1, d)]
    if last:
        return pl.pallas_call(
            functools.partial(_postln_last_kernel, alpha=alpha),
            grid=(rows // tm,), in_specs=in_specs, out_specs=tile,
            out_shape=jax.ShapeDtypeStruct((rows, d), F32),
            compiler_params=_params(1), name="postln_last",
        )(*args)
    in_specs += [_row_spec(mods, layer + 1, 0, gof), _row_spec(mods, layer + 1, 1, gof)]
    args += [mods, mods]
    return pl.pallas_call(
        functools.partial(_postln_next_kernel, alpha=alpha),
        grid=(rows // tm,), in_specs=in_specs, out_specs=[tile, tile],
        out_shape=[jax.ShapeDtypeStruct((rows, d), F32), jax.ShapeDtypeStruct((rows, d), BF16)],
        compiler_params=_params(1), name="postln_next",
    )(*args)


def _rot_cols(w):
    shp = w.shape
    wr = w.reshape(shp[:-1] + (2, 2, ROPE_FREQ))
    return jnp.stack([-wr[..., 1, :], wr[..., 0, :]], axis=-2).reshape(shp)


def _rope_tables(batch, seq, ctx_rows):
    rows = seq // GRID_W
    row = jnp.repeat(jnp.arange(rows, dtype=F32), GRID_W)
    col = jnp.tile(jnp.arange(GRID_W, dtype=F32), rows)
    pos = jnp.stack([row, col], axis=-1)
    inv_freq = ROPE_BASE ** (-(jnp.arange(ROPE_FREQ, dtype=F32) * 2.0) / (ROPE_DIM // 2))
    ang = pos[:, :, None] * inv_freq
    expand = lambda t: jnp.broadcast_to(t[:, :, None, :], (seq, 2, 2, ROPE_FREQ)).reshape(seq, ROPE_DIM)
    cos = jnp.concatenate([jnp.tile(expand(jnp.cos(ang)), (batch, 1)), jnp.ones((ctx_rows, ROPE_DIM), F32)])
    sin = jnp.concatenate([jnp.tile(expand(jnp.sin(ang)), (batch, 1)), jnp.zeros((ctx_rows, ROPE_DIM), F32)])
    pad = lambda t: jnp.pad(t, ((0, 0), (0, LANE - ROPE_DIM)))
    cos, sin = pad(cos), pad(sin)
    return cos, sin, cos.T, sin.T


def kernel(x, c, ctx, c_ctx, w_ada, b_ada, w_in, conv_w, q_norm_g, w_uq, kv_norm_g, w_ukv, w_out, ln_g, ln_b):
    batch, seq, d = x.shape
    ctx_len = ctx.shape[1]
    depth = w_ada.shape[0]
    cw = conv_w.shape[-1]
    heads = (w_out.shape[1] - cw) // HEAD_DIM
    ql, kvl = q_norm_g.shape[-1], kv_norm_g.shape[-1]
    lat_rows, ctx_rows = batch * seq, batch * ctx_len
    m_all = lat_rows + ctx_rows
    alpha = (2.0 * depth) ** 0.25

    sec = [cw, cw, cw, cw, ql, kvl, ROPE_DIM, cw]
    offs = [0]
    for s in sec:
        offs.append(offs[-1] + s)
    w_in_bf = w_in.astype(BF16)
    col = lambda k: w_in_bf[:, :, offs[k]:offs[k + 1]]
    w_mix = jnp.concatenate([w_in_bf[:, :, :offs[4]], col(7)], axis=-1)
    w_lora = jnp.concatenate([col(4), col(5), col(6), _rot_cols(col(6))], axis=-1)
    wq = w_uq.reshape(depth, ql, heads, HEAD_DIM + ROPE_DIM)
    wq = jnp.concatenate([wq, _rot_cols(wq[..., HEAD_DIM:])], axis=-1).reshape(depth, ql, heads * QK_PAD)
    wq_t = jnp.swapaxes(wq, 1, 2).astype(BF16)
    wkv = w_ukv.reshape(depth, kvl, heads, 2 * HEAD_DIM)
    wk = wkv[..., :HEAD_DIM].reshape(depth, kvl, heads * HEAD_DIM).astype(BF16)
    wv_t = jnp.swapaxes(wkv[..., HEAD_DIM:].reshape(depth, kvl, heads * HEAD_DIM), 1, 2).astype(BF16)
    wo_conv = w_out[:, :cw, :].astype(BF16)
    wo_mla = w_out[:, cw:, :].astype(BF16)
    tables = _rope_tables(batch, seq, ctx_rows)

    cc = jnp.zeros((SUBLANE, d), F32).at[:batch].set(c).at[batch].set(c_ctx)
    mods = _modulation(cc, w_ada, b_ada)[:, :batch + 1].reshape(depth, batch + 1, 3, 1, d)
    group_of = lambda i, tm: jnp.minimum((i * tm) // seq, batch)

    tm = _tile(math.gcd(seq, ctx_rows), (512, 256, 128))
    tm_small = _tile(math.gcd(seq, ctx_len), (256, 128))
    assert tm_small == ctx_len, "the context keys must form exactly one key chunk"
    x_all = jnp.concatenate([x.reshape(lat_rows, d), ctx.reshape(ctx_rows, d)], axis=0)
    h = _modulate(x_all, mods, 0, group_of, tm)

    for layer in range(depth):
        last = layer == depth - 1
        rows = lat_rows if last else m_all
        p_mix = _matmul(h, w_mix, layer, BF16, rows, tm, _tile(5 * cw, (1024, 512, 256, 128)), "in_proj_mix")
        p_lora = _matmul(h, w_lora, layer, F32, m_all, tm, w_lora.shape[-1], "in_proj_lora")
        q_t, k, v_t = _up_project(p_lora, tables, q_norm_g, kv_norm_g, wq_t, wk, wv_t, layer, heads, tm_small)
        y_attn = _attend_latent(q_t, k, v_t, batch, seq, ctx_len, heads, _tile(seq, (1024, 512, 256, 128)), tm_small)
        if not last:
            y_attn = jnp.concatenate([y_attn, _attend_ctx(q_t, k, v_t, batch, seq, ctx_len, heads)], axis=0)
        y_conv, y_mla = _merge(p_mix, y_attn, conv_w, layer, rows, lat_rows, seq, ctx_len,
                               tm_small, _tile(cw, (512, 256, 128)))
        out = _matmul2(y_conv, y_mla, wo_conv, wo_mla, layer, rows, tm, _tile(d, (1024, 512, 256, 128)), "out_proj")
        if last:
            x_all = _postln(x_all, out, mods, ln_g, ln_b, layer, rows, group_of, tm_small, alpha, True)
        else:
            x_all, h = _postln(x_all, out, mods, ln_g, ln_b, layer, rows, group_of, tm_small, alpha, False)
    return x_all.reshape(batch, seq, d)
```

```python
import functools
import math

import jax
import jax.numpy as jnp
from jax import lax
from jax.experimental import pallas as pl
from jax.experimental.pallas import tpu as pltpu

F32 = jnp.float32
BF16 = jnp.bfloat16

HEAD_DIM = 128
ROPE_DIM = 64
ROPE_FREQ = ROPE_DIM // 4
ROPE_BASE = 10000.0
GRID_W = 64
QK_PAD = 2 * HEAD_DIM
V_ROWS = HEAD_DIM + 16
EPS = 1e-6
ATTN_SCALE = 1.0 / math.sqrt(HEAD_DIM + ROPE_DIM)
Q_PRESCALE = ATTN_SCALE * math.log2(math.e)
ATTN_UNROLL = 9
LANE = 128
SUBLANE = 8
VMEM_LIMIT = 56 * 1024 * 1024


def _params(n_axes, vmem=VMEM_LIMIT):
    return pltpu.CompilerParams(dimension_semantics=("arbitrary",) * n_axes,
                                vmem_limit_bytes=vmem)


def _tile(n, prefs):
    for t in prefs:
        if n % t == 0:
            return t
    return n


def _silu(x):
    return x / (1.0 + jnp.exp(-x))


def _mod_kernel(a_ref, w_ref, b_ref, o_ref):
    a = _silu(a_ref[...]).astype(BF16)
    o_ref[...] = jnp.dot(a, w_ref[...].astype(BF16), preferred_element_type=F32) + b_ref[...]


def _modulation(cc, w_ada, b_ada):
    depth, d, n = w_ada.shape
    tn = _tile(n, (512, 256, 128))
    return pl.pallas_call(
        _mod_kernel,
        grid=(depth, n // tn),
        in_specs=[pl.BlockSpec((SUBLANE, d), lambda l, j: (0, 0)),
                  pl.BlockSpec((None, d, tn), lambda l, j: (l, 0, j)),
                  pl.BlockSpec((None, 1, tn), lambda l, j: (l, 0, j))],
        out_specs=pl.BlockSpec((None, SUBLANE, tn), lambda l, j: (l, 0, j)),
        out_shape=jax.ShapeDtypeStruct((depth, SUBLANE, n), F32),
        compiler_params=_params(2),
        name="adaln_mod",
    )(cc, w_ada, b_ada.reshape(depth, 1, n))


def _row_spec(mods, layer, part, group_of_tile):
    d = mods.shape[-1]
    return pl.BlockSpec((None, None, None, 1, d),
                        lambda i: (layer, group_of_tile(i), part, 0, 0))


def _modulate_kernel(x_ref, sh_ref, sc_ref, h_ref):
    h_ref[...] = (x_ref[...] * (1.0 + sc_ref[...]) + sh_ref[...]).astype(h_ref.dtype)


def _modulate(x_all, mods, layer, group_of, tm):
    m, d = x_all.shape
    gof = lambda i: group_of(i, tm)
    return pl.pallas_call(
        _modulate_kernel,
        grid=(m // tm,),
        in_specs=[pl.BlockSpec((tm, d), lambda i: (i, 0)),
                  _row_spec(mods, layer, 0, gof),
                  _row_spec(mods, layer, 1, gof)],
        out_specs=pl.BlockSpec((tm, d), lambda i: (i, 0)),
        out_shape=jax.ShapeDtypeStruct((m, d), BF16),
        compiler_params=_params(1),
        name="modulate",
    )(x_all, mods, mods)


def _mm_kernel(a_ref, b_ref, o_ref):
    o_ref[...] = jnp.dot(a_ref[...], b_ref[...], preferred_element_type=F32).astype(o_ref.dtype)


def _matmul(a, w, layer, out_dtype, rows, tm, tn, name):
    k = a.shape[1]
    n = w.shape[2]
    return pl.pallas_call(
        _mm_kernel,
        grid=(n // tn, rows // tm),
        in_specs=[pl.BlockSpec((tm, k), lambda j, i: (i, 0)),
                  pl.BlockSpec((None, k, tn), lambda j, i: (layer, 0, j))],
        out_specs=pl.BlockSpec((tm, tn), lambda j, i: (i, j)),
        out_shape=jax.ShapeDtypeStruct((rows, n), out_dtype),
        compiler_params=_params(2),
        name=name,
    )(a, w)


def _mm_f32w_kernel(a_ref, w_ref, o_ref, wb_ref):
    @pl.when(pl.program_id(1) == 0)
    def _():
        wb_ref[...] = w_ref[...].astype(BF16)

    o_ref[...] = jnp.dot(a_ref[...], wb_ref[...], preferred_element_type=F32).astype(o_ref.dtype)


def _matmul_f32w(a, w, layer, n_cols, out_dtype, rows, tm, tn, name):
    k = a.shape[1]
    return pl.pallas_call(
        _mm_f32w_kernel,
        grid=(n_cols // tn, rows // tm),
        in_specs=[pl.BlockSpec((tm, k), lambda j, i: (i, 0)),
                  pl.BlockSpec((None, k, tn), lambda j, i: (layer, 0, j))],
        out_specs=pl.BlockSpec((tm, tn), lambda j, i: (i, j)),
        out_shape=jax.ShapeDtypeStruct((rows, n_cols), out_dtype),
        scratch_shapes=[pltpu.VMEM((k, tn), BF16)],
        compiler_params=_params(2),
        name=name,
    )(a, w)


def _mm2_kernel(a1_ref, a2_ref, b1_ref, b2_ref, o_ref):
    acc = jnp.dot(a1_ref[...], b1_ref[...], preferred_element_type=F32)
    acc += jnp.dot(a2_ref[...], b2_ref[...], preferred_element_type=F32)
    o_ref[...] = acc.astype(o_ref.dtype)


def _matmul2(a1, a2, w1, w2, layer, rows, tm, tn, name):
    k = a1.shape[1]
    n = w1.shape[2]
    return pl.pallas_call(
        _mm2_kernel,
        grid=(n // tn, rows // tm),
        in_specs=[pl.BlockSpec((tm, k), lambda j, i: (i, 0)),
                  pl.BlockSpec((tm, k), lambda j, i: (i, 0)),
                  pl.BlockSpec((None, k, tn), lambda j, i: (layer, 0, j)),
                  pl.BlockSpec((None, k, tn), lambda j, i: (layer, 0, j))],
        out_specs=pl.BlockSpec((tm, tn), lambda j, i: (i, j)),
        out_shape=jax.ShapeDtypeStruct((rows, n), F32),
        compiler_params=_params(2),
        name=name,
    )(a1, a2, w1, w2)


def _rope_half(x, cos, sin):
    return x * cos + pltpu.roll(x, ROPE_DIM, axis=1) * sin


def _rms(x, g):
    return (x * lax.rsqrt(jnp.mean(x * x, axis=-1, keepdims=True) + EPS)) * g


def _dot_nt(a, b):
    return lax.dot_general(a, b, (((1,), (1,)), ((), ())), preferred_element_type=F32)


def _up_kernel(p_ref, cos_ref, sin_ref, cos_t_ref, sin_t_ref, gq_ref, gkv_ref, wq_t_ref, wk_ref, wv_t_ref,
               q_t_ref, k_ref, v_t_ref, *, ql, kvl, heads):
    cq = _rms(p_ref[:, :ql], gq_ref[...]).astype(BF16)
    ckv = _rms(p_ref[:, ql:ql + kvl], gkv_ref[...]).astype(BF16)
    krp = _rope_half(p_ref[:, ql + kvl:], cos_ref[...], sin_ref[...]).astype(BF16)
    cos_t = cos_t_ref[...] * Q_PRESCALE
    sin_t = sin_t_ref[...] * Q_PRESCALE
    q_t = _dot_nt(wq_t_ref[...], cq)
    kn = jnp.dot(ckv, wk_ref[...], preferred_element_type=F32)
    for h in range(heads):
        lo = h * QK_PAD
        hi = q_t[lo + HEAD_DIM:lo + QK_PAD, :]
        q_t_ref[lo:lo + HEAD_DIM, :] = (q_t[lo:lo + HEAD_DIM, :] * Q_PRESCALE).astype(BF16)
        q_t_ref[lo + HEAD_DIM:lo + QK_PAD, :] = (
            hi * cos_t + pltpu.roll(hi, ROPE_DIM, axis=0) * sin_t).astype(BF16)
        k_ref[:, lo:lo + HEAD_DIM] = kn[:, h * HEAD_DIM:(h + 1) * HEAD_DIM].astype(BF16)
        k_ref[:, lo + HEAD_DIM:lo + QK_PAD] = krp
    v_t = _dot_nt(wv_t_ref[...], ckv)
    for h in range(heads):
        lo = h * V_ROWS
        v_t_ref[lo:lo + HEAD_DIM, :] = v_t[h * HEAD_DIM:(h + 1) * HEAD_DIM, :].astype(BF16)
        v_t_ref[lo + HEAD_DIM:lo + V_ROWS, :] = jnp.ones((V_ROWS - HEAD_DIM, v_t.shape[1]), BF16)


def _up_project(p_lora, tables, gq, gkv, wq_t, wk, wv_t, layer, heads, tm):
    m, pw = p_lora.shape
    ql, kvl = gq.shape[-1], gkv.shape[-1]
    depth = wq_t.shape[0]
    cos_r, sin_r, cos_t, sin_t = tables
    const = lambda i: (layer, 0, 0)
    return pl.pallas_call(
        functools.partial(_up_kernel, ql=ql, kvl=kvl, heads=heads),
        grid=(m // tm,),
        in_specs=[pl.BlockSpec((tm, pw), lambda i: (i, 0)),
                  pl.BlockSpec((tm, LANE), lambda i: (i, 0)),
                  pl.BlockSpec((tm, LANE), lambda i: (i, 0)),
                  pl.BlockSpec((LANE, tm), lambda i: (0, i)),
                  pl.BlockSpec((LANE, tm), lambda i: (0, i)),
                  pl.BlockSpec((None, 1, ql), const),
                  pl.BlockSpec((None, 1, kvl), const),
                  pl.BlockSpec((None, heads * QK_PAD, ql), const),
                  pl.BlockSpec((None, kvl, heads * HEAD_DIM), const),
                  pl.BlockSpec((None, heads * HEAD_DIM, kvl), const)],
        out_specs=[pl.BlockSpec((heads * QK_PAD, tm), lambda i: (0, i)),
                   pl.BlockSpec((tm, heads * QK_PAD), lambda i: (i, 0)),
                   pl.BlockSpec((None, heads * V_ROWS, tm), lambda i: (i, 0, 0))],
        out_shape=[jax.ShapeDtypeStruct((heads * QK_PAD, m), BF16),
                   jax.ShapeDtypeStruct((m, heads * QK_PAD), BF16),
                   jax.ShapeDtypeStruct((m // tm, heads * V_ROWS, tm), BF16)],
        compiler_params=_params(1),
        name="lora_up",
    )(p_lora, cos_r, sin_r, cos_t, sin_t, gq.reshape(depth, 1, ql), gkv.reshape(depth, 1, kvl), wq_t, wk, wv_t)


def _attn_kernel(q_t_ref, kl_ref, kc_ref, vl_ref, vc_ref, o_ref,
                 s0_ref, s1_ref, s2_ref, p0_ref, p1_ref, p2_ref, a0_ref, a1_ref, a2_ref,
                 c0_ref, c1_ref, c2_ref, m_ref, acc_ref, *, tk, unroll):
    n_lat = kl_ref.shape[0] // tk
    n = n_lat + 1
    s_buf, p_buf, a_buf = (s0_ref, s1_ref, s2_ref), (p0_ref, p1_ref, p2_ref), (a0_ref, a1_ref, a2_ref)
    c_buf = (c0_ref, c1_ref, c2_ref)
    q_t = q_t_ref[...]

    def k_chunk(c):
        if isinstance(c, int) and c == n_lat:
            return kc_ref[...]
        return kl_ref[pl.ds(pl.multiple_of(c * tk, tk), tk), :]

    def v_chunk(c):
        if isinstance(c, int) and c == n_lat:
            return vc_ref[0]
        return vl_ref[c]

    def scores(c, slot):
        s = jnp.dot(k_chunk(c), q_t, preferred_element_type=F32)
        s_buf[slot][...] = s
        c_buf[slot][...] = jnp.max(s, axis=0, keepdims=True)

    def softmax(slot):
        m_old = m_ref[...]
        m_new = jnp.maximum(m_old, c_buf[slot][...])
        alpha = jnp.exp2(m_old - m_new)
        m_ref[...] = m_new
        a_buf[slot][...] = alpha
        p_buf[slot][...] = jnp.exp2((s_buf[slot][...] - m_new).astype(BF16))

    def values(c, slot):
        acc_ref[...] = a_buf[slot][...] * acc_ref[...] + jnp.dot(
            v_chunk(c), p_buf[slot][...], preferred_element_type=F32)

    def stage(c, slot, with_scores=True, with_values=True):
        if with_scores:
            scores(c + 2, (slot + 2) % 3)
        softmax(slot)
        if with_values:
            values(c - 2, (slot + 1) % 3)

    m_ref[...] = jnp.full(m_ref.shape, -jnp.inf, F32)
    acc_ref[...] = jnp.zeros(acc_ref.shape, F32)
    scores(0, 0)
    scores(1, 1)
    for c in range(2):
        stage(c, c, with_scores=c + 2 < n, with_values=False)
    n_trips = max(n - 5, 0) // unroll

    def trip(j, carry):
        for r in range(unroll):
            stage(2 + unroll * j + r, (2 + r) % 3)
        return carry

    lax.fori_loop(0, n_trips, trip, 0)
    for c in range(2 + unroll * n_trips, n):
        stage(c, c % 3, with_scores=c + 2 < n)
    values(n - 2, (n - 2) % 3)
    values(n - 1, (n - 1) % 3)
    o_t = acc_ref[:HEAD_DIM, :] / acc_ref[HEAD_DIM:HEAD_DIM + 1, :]
    o_ref[...] = o_t.T.astype(o_ref.dtype)


def _attend_latent(q_t, k, v_t, batch, seq, ctx_len, heads, tq, tk):
    n_q = seq // tq
    ctx0 = batch * seq // ctx_len
    n_lat = seq // tk
    return pl.pallas_call(
        functools.partial(_attn_kernel, tk=tk, unroll=ATTN_UNROLL),
        grid=(batch, heads, n_q),
        in_specs=[pl.BlockSpec((QK_PAD, tq), lambda b, h, i: (h, b * n_q + i)),
                  pl.BlockSpec((seq, QK_PAD), lambda b, h, i: (b, h)),
                  pl.BlockSpec((ctx_len, QK_PAD), lambda b, h, i: (ctx0 + b, h)),
                  pl.BlockSpec((n_lat, V_ROWS, tk), lambda b, h, i: (b, h, 0)),
                  pl.BlockSpec((ctx_len // tk, V_ROWS, tk), lambda b, h, i: (ctx0 + b, h, 0))],
        out_specs=pl.BlockSpec((tq, HEAD_DIM), lambda b, h, i: (b * n_q + i, h)),
        out_shape=jax.ShapeDtypeStruct((batch * seq, heads * HEAD_DIM), BF16),
        scratch_shapes=[pltpu.VMEM((tk, tq), F32)] * 3 + [pltpu.VMEM((tk, tq), BF16)] * 3 + [
                        pltpu.VMEM((1, tq), F32)] * 6 + [
                        pltpu.VMEM((1, tq), F32), pltpu.VMEM((V_ROWS, tq), F32)],
        compiler_params=_params(3),
        name="attn_latent",
    )(q_t, k, k, v_t, v_t)


def _attn_ctx_kernel(q_t_ref, k_ref, v_t_ref, o_ref):
    s = jnp.dot(k_ref[...], q_t_ref[...], preferred_element_type=F32)
    p = jnp.exp2((s - jnp.max(s, axis=0, keepdims=True)).astype(BF16))
    o_t = jnp.dot(v_t_ref[0], p, preferred_element_type=F32)
    o_ref[...] = (o_t[:HEAD_DIM, :] / o_t[HEAD_DIM:HEAD_DIM + 1, :]).T.astype(o_ref.dtype)


def _attend_ctx(q_t, k, v_t, batch, seq, ctx_len, heads):
    ctx0 = batch * seq // ctx_len
    return pl.pallas_call(
        _attn_ctx_kernel,
        grid=(batch, heads),
        in_specs=[pl.BlockSpec((QK_PAD, ctx_len), lambda b, h: (h, ctx0 + b)),
                  pl.BlockSpec((ctx_len, QK_PAD), lambda b, h: (ctx0 + b, h)),
                  pl.BlockSpec((1, V_ROWS, ctx_len), lambda b, h: (ctx0 + b, h, 0))],
        out_specs=pl.BlockSpec((ctx_len, HEAD_DIM), lambda b, h: (b, h)),
        out_shape=jax.ShapeDtypeStruct((batch * ctx_len, heads * HEAD_DIM), BF16),
        compiler_params=_params(2),
        name="attn_ctx",
    )(q_t, k, v_t)


def _merge_kernel(u_ref, bg_ref, cg_ref, zc_ref, zm_ref, up_ref, cp_ref, un_ref, cn_ref,
                  ya_ref, cw_ref, yc_ref, ym_ref, *, tm, lat_rows, seq, ctx_len):
    i = pl.program_id(0)
    row0 = i * tm
    row1 = row0 + tm
    is_start = jnp.where(row0 < lat_rows, row0 % seq == 0, (row0 - lat_rows) % ctx_len == 0)
    is_end = jnp.where(row1 <= lat_rows, row1 % seq == 0, (row1 - lat_rows) % ctx_len == 0)
    v = cg_ref[...].astype(F32) * u_ref[...].astype(F32)
    v_before = (cp_ref[SUBLANE - 1:SUBLANE, :].astype(F32) * up_ref[SUBLANE - 1:SUBLANE, :].astype(F32))
    v_after = cn_ref[0:1, :].astype(F32) * un_ref[0:1, :].astype(F32)
    v_before = jnp.where(is_start, 0.0, v_before)
    v_after = jnp.where(is_end, 0.0, v_after)
    row = lax.broadcasted_iota(jnp.int32, v.shape, 0)
    v_prev = jnp.where(row == 0, v_before, pltpu.roll(v, 1, axis=0))
    v_next = jnp.where(row == tm - 1, v_after, pltpu.roll(v, tm - 1, axis=0))
    cw = cw_ref[...]
    y = v_prev * cw[0:1, :] + v * cw[1:2, :] + v_next * cw[2:3, :]
    yc_ref[...] = (bg_ref[...].astype(F32) * y * _silu(zc_ref[...].astype(F32))).astype(yc_ref.dtype)
    ym_ref[...] = (ya_ref[...].astype(F32) * _silu(zm_ref[...].astype(F32))).astype(ym_ref.dtype)


def _merge(p_conv, p_zm, y_attn, conv_w, layer, rows, lat_rows, seq, ctx_len, tm, tc):
    m_all = p_conv.shape[0]
    cw = conv_w.shape[-1]
    nc = cw // tc
    hb = tm // SUBLANE
    last_hb = m_all // SUBLANE - 1
    main = lambda s: pl.BlockSpec((tm, tc), lambda i, j: (i, s * nc + j))
    prev = lambda s: pl.BlockSpec((SUBLANE, tc), lambda i, j: (jnp.maximum(i * hb - 1, 0), s * nc + j))
    nxt = lambda s: pl.BlockSpec((SUBLANE, tc), lambda i, j: (jnp.minimum((i + 1) * hb, last_hb), s * nc + j))
    out = pl.BlockSpec((tm, tc), lambda i, j: (i, j))
    return pl.pallas_call(
        functools.partial(_merge_kernel, tm=tm, lat_rows=lat_rows, seq=seq, ctx_len=ctx_len),
        grid=(rows // tm, nc),
        in_specs=[main(0), main(1), main(2), main(3), out,
                  prev(0), prev(2), nxt(0), nxt(2),
                  out,
                  pl.BlockSpec((None, 3, tc), lambda i, j: (layer, 0, j))],
        out_specs=[out, out],
        out_shape=[jax.ShapeDtypeStruct((rows, cw), BF16), jax.ShapeDtypeStruct((rows, cw), BF16)],
        compiler_params=_params(2),
        name="conv_merge",
    )(p_conv, p_conv, p_conv, p_conv, p_zm, p_conv, p_conv, p_conv, p_conv, y_attn, conv_w)


def _layernorm(z, g, b):
    mu = jnp.mean(z, axis=-1, keepdims=True)
    zc = z - mu
    var = jnp.mean(zc * zc, axis=-1, keepdims=True)
    return zc * lax.rsqrt(var + EPS) * g + b


def _postln_next_kernel(x_ref, o_ref, gate_ref, g_ref, b_ref, sh_ref, sc_ref, xn_ref, h_ref, *, alpha):
    xn = _layernorm(alpha * x_ref[...] + gate_ref[...] * o_ref[...], g_ref[...], b_ref[...])
    xn_ref[...] = xn
    h_ref[...] = (xn * (1.0 + sc_ref[...]) + sh_ref[...]).astype(h_ref.dtype)


def _postln_last_kernel(x_ref, o_ref, gate_ref, g_ref, b_ref, xn_ref, *, alpha):
    xn_ref[...] = _layernorm(alpha * x_ref[...] + gate_ref[...] * o_ref[...], g_ref[...], b_ref[...])


def _postln(x_all, out, mods, ln_g, ln_b, layer, rows, group_of, tm, alpha, last):
    d = x_all.shape[1]
    depth = ln_g.shape[0]
    gof = lambda i: group_of(i, tm)
    tile = pl.BlockSpec((tm, d), lambda i: (i, 0))
    vec = pl.BlockSpec((None, 1, d), lambda i: (layer, 0, 0))
    in_specs = [tile, tile, _row_spec(mods, layer, 2, gof), vec, vec]
    args = [x_all, out, mods, ln_g.reshape(depth, 1, d), ln_b.reshape(depth, 1, d)]
    if last:
        return pl.pallas_call(
            functools.partial(_postln_last_kernel, alpha=alpha),
            grid=(rows // tm,), in_specs=in_specs, out_specs=tile,
            out_shape=jax.ShapeDtypeStruct((rows, d), F32),
            compiler_params=_params(1), name="postln_last",
        )(*args)
    in_specs += [_row_spec(mods, layer + 1, 0, gof), _row_spec(mods, layer + 1, 1, gof)]
    args += [mods, mods]
    return pl.pallas_call(
        functools.partial(_postln_next_kernel, alpha=alpha),
        grid=(rows // tm,), in_specs=in_specs, out_specs=[tile, tile],
        out_shape=[jax.ShapeDtypeStruct((rows, d), F32), jax.ShapeDtypeStruct((rows, d), BF16)],
        compiler_params=_params(1), name="postln_next",
    )(*args)


def _rot_cols(w):
    shp = w.shape
    wr = w.reshape(shp[:-1] + (2, 2, ROPE_FREQ))
    return jnp.stack([-wr[..., 1, :], wr[..., 0, :]], axis=-2).reshape(shp)


def _rope_tables(batch, seq, ctx_rows):
    rows = seq // GRID_W
    row = jnp.repeat(jnp.arange(rows, dtype=F32), GRID_W)
    col = jnp.tile(jnp.arange(GRID_W, dtype=F32), rows)
    pos = jnp.stack([row, col], axis=-1)
    inv_freq = ROPE_BASE ** (-(jnp.arange(ROPE_FREQ, dtype=F32) * 2.0) / (ROPE_DIM // 2))
    ang = pos[:, :, None] * inv_freq
    expand = lambda t: jnp.broadcast_to(t[:, :, None, :], (seq, 2, 2, ROPE_FREQ)).reshape(seq, ROPE_DIM)
    cos = jnp.concatenate([jnp.tile(expand(jnp.cos(ang)), (batch, 1)), jnp.ones((ctx_rows, ROPE_DIM), F32)])
    sin = jnp.concatenate([jnp.tile(expand(jnp.sin(ang)), (batch, 1)), jnp.zeros((ctx_rows, ROPE_DIM), F32)])
    pad = lambda t: jnp.pad(t, ((0, 0), (0, LANE - ROPE_DIM)))
    cos, sin = pad(cos), pad(sin)
    return cos, sin, cos.T, sin.T


def kernel(x, c, ctx, c_ctx, w_ada, b_ada, w_in, conv_w, q_norm_g, w_uq, kv_norm_g, w_ukv, w_out, ln_g, ln_b):
    batch, seq, d = x.shape
    ctx_len = ctx.shape[1]
    depth = w_ada.shape[0]
    cw = conv_w.shape[-1]
    heads = (w_out.shape[1] - cw) // HEAD_DIM
    ql, kvl = q_norm_g.shape[-1], kv_norm_g.shape[-1]
    lat_rows, ctx_rows = batch * seq, batch * ctx_len
    m_all = lat_rows + ctx_rows
    alpha = (2.0 * depth) ** 0.25

    sec = [cw, cw, cw, cw, ql, kvl, ROPE_DIM, cw]
    offs = [0]
    for s in sec:
        offs.append(offs[-1] + s)
    col = lambda k: w_in[:, :, offs[k]:offs[k + 1]]
    w_zm = col(7).astype(BF16)
    w_lora = jnp.concatenate([w_in[:, :, offs[4]:offs[7]], _rot_cols(col(6))], axis=-1).astype(BF16)
    wq = w_uq.reshape(depth, ql, heads, HEAD_DIM + ROPE_DIM)
    wq = jnp.concatenate([wq, _rot_cols(wq[..., HEAD_DIM:])], axis=-1).reshape(depth, ql, heads * QK_PAD)
    wq_t = jnp.swapaxes(wq, 1, 2).astype(BF16)
    wkv = w_ukv.reshape(depth, kvl, heads, 2 * HEAD_DIM)
    wk = wkv[..., :HEAD_DIM].reshape(depth, kvl, heads * HEAD_DIM).astype(BF16)
    wv_t = jnp.swapaxes(wkv[..., HEAD_DIM:].reshape(depth, kvl, heads * HEAD_DIM), 1, 2).astype(BF16)
    wo_conv = w_out[:, :cw, :].astype(BF16)
    wo_mla = w_out[:, cw:, :].astype(BF16)
    tables = _rope_tables(batch, seq, ctx_rows)

    cc = jnp.zeros((SUBLANE, d), F32).at[:batch].set(c).at[batch].set(c_ctx)
    mods = _modulation(cc, w_ada, b_ada)[:, :batch + 1].reshape(depth, batch + 1, 3, 1, d)
    group_of = lambda i, tm: jnp.minimum((i * tm) // seq, batch)

    tm = _tile(math.gcd(seq, ctx_rows), (512, 256, 128))
    tm_small = _tile(math.gcd(seq, ctx_len), (256, 128))
    assert tm_small == ctx_len, "the context keys must form exactly one key chunk"
    x_all = jnp.concatenate([x.reshape(lat_rows, d), ctx.reshape(ctx_rows, d)], axis=0)
    h = _modulate(x_all, mods, 0, group_of, tm)

    for layer in range(depth):
        last = layer == depth - 1
        rows = lat_rows if last else m_all
        tn = _tile(cw, (1024, 512, 256, 128))
        p_conv = _matmul_f32w(h, w_in, layer, 4 * cw, BF16, rows, tm, tn, "in_proj_conv")
        p_zm = _matmul(h, w_zm, layer, BF16, rows, tm, tn, "in_proj_zm")
        p_lora = _matmul(h, w_lora, layer, F32, m_all, tm, w_lora.shape[-1], "in_proj_lora")
        q_t, k, v_t = _up_project(p_lora, tables, q_norm_g, kv_norm_g, wq_t, wk, wv_t, layer, heads, tm_small)
        y_attn = _attend_latent(q_t, k, v_t, batch, seq, ctx_len, heads, _tile(seq, (2048, 1024, 512, 256, 128)), tm_small)
        if not last:
            y_attn = jnp.concatenate([y_attn, _attend_ctx(q_t, k, v_t, batch, seq, ctx_len, heads)], axis=0)
        y_conv, y_mla = _merge(p_conv, p_zm, y_attn, conv_w, layer, rows, lat_rows, seq, ctx_len,
                               tm_small, _tile(cw, (512, 256, 128)))
        out = _matmul2(y_conv, y_mla, wo_conv, wo_mla, layer, rows, tm, _tile(d, (1024, 512, 256, 128)), "out_proj")
        if last:
            x_all = _postln(x_all, out, mods, ln_g, ln_b, layer, rows, group_of, tm_small, alpha, True)
        else:
            x_all, h = _postln(x_all, out, mods, ln_g, ln_b, layer, rows, group_of, tm_small, alpha, False)
    return x_all.reshape(batch, seq, d)
```

```python
import functools
import math

import jax
import jax.numpy as jnp
from jax import lax
from jax.experimental import pallas as pl
from jax.experimental.pallas import tpu as pltpu

F32 = jnp.float32
BF16 = jnp.bfloat16

HEAD_DIM = 128
ROPE_DIM = 64
ROPE_FREQ = ROPE_DIM // 4
ROPE_BASE = 10000.0
GRID_W = 64
QK_PAD = 2 * HEAD_DIM
V_ROWS = HEAD_DIM + 16
EPS = 1e-6
ATTN_SCALE = 1.0 / math.sqrt(HEAD_DIM + ROPE_DIM)
Q_PRESCALE = ATTN_SCALE * math.log2(math.e)
ATTN_UNROLL = 9
LANE = 128
SUBLANE = 8
VMEM_LIMIT = 56 * 1024 * 1024


def _params(n_axes, vmem=VMEM_LIMIT):
    return pltpu.CompilerParams(dimension_semantics=("arbitrary",) * n_axes,
                                vmem_limit_bytes=vmem)


def _tile(n, prefs):
    for t in prefs:
        if n % t == 0:
            return t
    return n


def _silu(x):
    return x / (1.0 + jnp.exp(-x))


def _mod_kernel(a_ref, w_ref, b_ref, o_ref):
    a = _silu(a_ref[...]).astype(BF16)
    o_ref[...] = jnp.dot(a, w_ref[...].astype(BF16), preferred_element_type=F32) + b_ref[...]


def _modulation(cc, w_ada, b_ada):
    depth, d, n = w_ada.shape
    tn = _tile(n, (512, 256, 128))
    return pl.pallas_call(
        _mod_kernel,
        grid=(depth, n // tn),
        in_specs=[pl.BlockSpec((SUBLANE, d), lambda l, j: (0, 0)),
                  pl.BlockSpec((None, d, tn), lambda l, j: (l, 0, j)),
                  pl.BlockSpec((None, 1, tn), lambda l, j: (l, 0, j))],
        out_specs=pl.BlockSpec((None, SUBLANE, tn), lambda l, j: (l, 0, j)),
        out_shape=jax.ShapeDtypeStruct((depth, SUBLANE, n), F32),
        compiler_params=_params(2),
        name="adaln_mod",
    )(cc, w_ada, b_ada.reshape(depth, 1, n))


def _row_spec(mods, layer, part, group_of_tile):
    d = mods.shape[-1]
    return pl.BlockSpec((None, None, None, 1, d),
                        lambda i: (layer, group_of_tile(i), part, 0, 0))


def _modulate_kernel(x_ref, sh_ref, sc_ref, h_ref):
    h_ref[...] = (x_ref[...] * (1.0 + sc_ref[...]) + sh_ref[...]).astype(h_ref.dtype)


def _modulate(x_all, mods, layer, group_of, tm):
    m, d = x_all.shape
    gof = lambda i: group_of(i, tm)
    return pl.pallas_call(
        _modulate_kernel,
        grid=(m // tm,),
        in_specs=[pl.BlockSpec((tm, d), lambda i: (i, 0)),
                  _row_spec(mods, layer, 0, gof),
                  _row_spec(mods, layer, 1, gof)],
        out_specs=pl.BlockSpec((tm, d), lambda i: (i, 0)),
        out_shape=jax.ShapeDtypeStruct((m, d), BF16),
        compiler_params=_params(1),
        name="modulate",
    )(x_all, mods, mods)


def _mm_kernel(a_ref, b_ref, o_ref):
    o_ref[...] = jnp.dot(a_ref[...], b_ref[...], preferred_element_type=F32).astype(o_ref.dtype)


def _matmul(a, w, layer, out_dtype, rows, tm, tn, name):
    k = a.shape[1]
    n = w.shape[2]
    return pl.pallas_call(
        _mm_kernel,
        grid=(n // tn, rows // tm),
        in_specs=[pl.BlockSpec((tm, k), lambda j, i: (i, 0)),
                  pl.BlockSpec((None, k, tn), lambda j, i: (layer, 0, j))],
        out_specs=pl.BlockSpec((tm, tn), lambda j, i: (i, j)),
        out_shape=jax.ShapeDtypeStruct((rows, n), out_dtype),
        compiler_params=_params(2),
        name=name,
    )(a, w)


def _mm2_kernel(a1_ref, a2_ref, b1_ref, b2_ref, o_ref):
    acc = jnp.dot(a1_ref[...], b1_ref[...], preferred_element_type=F32)
    acc += jnp.dot(a2_ref[...], b2_ref[...], preferred_element_type=F32)
    o_ref[...] = acc.astype(o_ref.dtype)


def _matmul2(a1, a2, w1, w2, layer, rows, tm, tn, name):
    k = a1.shape[1]
    n = w1.shape[2]
    return pl.pallas_call(
        _mm2_kernel,
        grid=(n // tn, rows // tm),
        in_specs=[pl.BlockSpec((tm, k), lambda j, i: (i, 0)),
                  pl.BlockSpec((tm, k), lambda j, i: (i, 0)),
                  pl.BlockSpec((None, k, tn), lambda j, i: (layer, 0, j)),
                  pl.BlockSpec((None, k, tn), lambda j, i: (layer, 0, j))],
        out_specs=pl.BlockSpec((tm, tn), lambda j, i: (i, j)),
        out_shape=jax.ShapeDtypeStruct((rows, n), F32),
        compiler_params=_params(2),
        name=name,
    )(a1, a2, w1, w2)


def _rope_half(x, cos, sin):
    return x * cos + pltpu.roll(x, ROPE_DIM, axis=1) * sin


def _rms(x, g):
    return (x * lax.rsqrt(jnp.mean(x * x, axis=-1, keepdims=True) + EPS)) * g


def _dot_nt(a, b):
    return lax.dot_general(a, b, (((1,), (1,)), ((), ())), preferred_element_type=F32)


def _up_kernel(p_ref, cos_ref, sin_ref, cos_t_ref, sin_t_ref, gq_ref, gkv_ref, wq_t_ref, wk_ref, wv_t_ref,
               q_t_ref, k_ref, v_t_ref, *, ql, kvl, heads):
    cq = _rms(p_ref[:, :ql], gq_ref[...]).astype(BF16)
    ckv = _rms(p_ref[:, ql:ql + kvl], gkv_ref[...]).astype(BF16)
    krp = _rope_half(p_ref[:, ql + kvl:], cos_ref[...], sin_ref[...]).astype(BF16)
    cos_t = cos_t_ref[...] * Q_PRESCALE
    sin_t = sin_t_ref[...] * Q_PRESCALE
    q_t = _dot_nt(wq_t_ref[...], cq)
    kn = jnp.dot(ckv, wk_ref[...], preferred_element_type=F32)
    for h in range(heads):
        lo = h * QK_PAD
        hi = q_t[lo + HEAD_DIM:lo + QK_PAD, :]
        q_t_ref[lo:lo + HEAD_DIM, :] = (q_t[lo:lo + HEAD_DIM, :] * Q_PRESCALE).astype(BF16)
        q_t_ref[lo + HEAD_DIM:lo + QK_PAD, :] = (
            hi * cos_t + pltpu.roll(hi, ROPE_DIM, axis=0) * sin_t).astype(BF16)
        k_ref[:, lo:lo + HEAD_DIM] = kn[:, h * HEAD_DIM:(h + 1) * HEAD_DIM].astype(BF16)
        k_ref[:, lo + HEAD_DIM:lo + QK_PAD] = krp
    v_t = _dot_nt(wv_t_ref[...], ckv)
    for h in range(heads):
        lo = h * V_ROWS
        v_t_ref[lo:lo + HEAD_DIM, :] = v_t[h * HEAD_DIM:(h + 1) * HEAD_DIM, :].astype(BF16)
        v_t_ref[lo + HEAD_DIM:lo + V_ROWS, :] = jnp.ones((V_ROWS - HEAD_DIM, v_t.shape[1]), BF16)


def _up_project(p_lora, tables, gq, gkv, wq_t, wk, wv_t, layer, heads, tm):
    m, pw = p_lora.shape
    ql, kvl = gq.shape[-1], gkv.shape[-1]
    depth = wq_t.shape[0]
    cos_r, sin_r, cos_t, sin_t = tables
    const = lambda i: (layer, 0, 0)
    return pl.pallas_call(
        functools.partial(_up_kernel, ql=ql, kvl=kvl, heads=heads),
        grid=(m // tm,),
        in_specs=[pl.BlockSpec((tm, pw), lambda i: (i, 0)),
                  pl.BlockSpec((tm, LANE), lambda i: (i, 0)),
                  pl.BlockSpec((tm, LANE), lambda i: (i, 0)),
                  pl.BlockSpec((LANE, tm), lambda i: (0, i)),
                  pl.BlockSpec((LANE, tm), lambda i: (0, i)),
                  pl.BlockSpec((None, 1, ql), const),
                  pl.BlockSpec((None, 1, kvl), const),
                  pl.BlockSpec((None, heads * QK_PAD, ql), const),
                  pl.BlockSpec((None, kvl, heads * HEAD_DIM), const),
                  pl.BlockSpec((None, heads * HEAD_DIM, kvl), const)],
        out_specs=[pl.BlockSpec((heads * QK_PAD, tm), lambda i: (0, i)),
                   pl.BlockSpec((tm, heads * QK_PAD), lambda i: (i, 0)),
                   pl.BlockSpec((None, heads * V_ROWS, tm), lambda i: (i, 0, 0))],
        out_shape=[jax.ShapeDtypeStruct((heads * QK_PAD, m), BF16),
                   jax.ShapeDtypeStruct((m, heads * QK_PAD), BF16),
                   jax.ShapeDtypeStruct((m // tm, heads * V_ROWS, tm), BF16)],
        compiler_params=_params(1),
        name="lora_up",
    )(p_lora, cos_r, sin_r, cos_t, sin_t, gq.reshape(depth, 1, ql), gkv.reshape(depth, 1, kvl), wq_t, wk, wv_t)


def _attn_kernel(q_t_ref, kl_ref, kc_ref, vl_ref, vc_ref, o_ref,
                 s0_ref, s1_ref, s2_ref, p0_ref, p1_ref, p2_ref, a0_ref, a1_ref, a2_ref,
                 c0_ref, c1_ref, c2_ref, m_ref, acc_ref, *, tk, unroll):
    n_lat = kl_ref.shape[0] // tk
    n = n_lat + 1
    s_buf, p_buf, a_buf = (s0_ref, s1_ref, s2_ref), (p0_ref, p1_ref, p2_ref), (a0_ref, a1_ref, a2_ref)
    c_buf = (c0_ref, c1_ref, c2_ref)
    q_t = q_t_ref[...]

    def k_chunk(c):
        if isinstance(c, int) and c == n_lat:
            return kc_ref[...]
        return kl_ref[pl.ds(pl.multiple_of(c * tk, tk), tk), :]

    def v_chunk(c):
        if isinstance(c, int) and c == n_lat:
            return vc_ref[0]
        return vl_ref[c]

    def scores(c, slot):
        s = jnp.dot(k_chunk(c), q_t, preferred_element_type=F32)
        s_buf[slot][...] = s
        c_buf[slot][...] = jnp.max(s, axis=0, keepdims=True)

    def softmax(slot):
        m_old = m_ref[...]
        m_new = jnp.maximum(m_old, c_buf[slot][...])
        alpha = jnp.exp2(m_old - m_new)
        m_ref[...] = m_new
        a_buf[slot][...] = alpha
        p_buf[slot][...] = jnp.exp2((s_buf[slot][...] - m_new).astype(BF16))

    def values(c, slot):
        acc_ref[...] = a_buf[slot][...] * acc_ref[...] + jnp.dot(
            v_chunk(c), p_buf[slot][...], preferred_element_type=F32)

    def stage(c, slot, with_scores=True, with_values=True):
        if with_scores:
            scores(c + 2, (slot + 2) % 3)
        if with_values:
            values(c - 2, (slot + 1) % 3)
        softmax(slot)

    m_ref[...] = jnp.full(m_ref.shape, -jnp.inf, F32)
    acc_ref[...] = jnp.zeros(acc_ref.shape, F32)
    scores(0, 0)
    scores(1, 1)
    for c in range(2):
        stage(c, c, with_scores=c + 2 < n, with_values=False)
    n_trips = max(n - 5, 0) // unroll

    def trip(j, carry):
        for r in range(unroll):
            stage(2 + unroll * j + r, (2 + r) % 3)
        return carry

    lax.fori_loop(0, n_trips, trip, 0)
    for c in range(2 + unroll * n_trips, n):
        stage(c, c % 3, with_scores=c + 2 < n)
    values(n - 2, (n - 2) % 3)
    values(n - 1, (n - 1) % 3)
    o_t = acc_ref[:HEAD_DIM, :] / acc_ref[HEAD_DIM:HEAD_DIM + 1, :]
    o_ref[...] = o_t.T.astype(o_ref.dtype)


def _attend_latent(q_t, k, v_t, batch, seq, ctx_len, heads, tq, tk):
    n_q = seq // tq
    ctx0 = batch * seq // ctx_len
    n_lat = seq // tk
    return pl.pallas_call(
        functools.partial(_attn_kernel, tk=tk, unroll=ATTN_UNROLL),
        grid=(batch, heads, n_q),
        in_specs=[pl.BlockSpec((QK_PAD, tq), lambda b, h, i: (h, b * n_q + i)),
                  pl.BlockSpec((seq, QK_PAD), lambda b, h, i: (b, h)),
                  pl.BlockSpec((ctx_len, QK_PAD), lambda b, h, i: (ctx0 + b, h)),
                  pl.BlockSpec((n_lat, V_ROWS, tk), lambda b, h, i: (b, h, 0)),
                  pl.BlockSpec((ctx_len // tk, V_ROWS, tk), lambda b, h, i: (ctx0 + b, h, 0))],
        out_specs=pl.BlockSpec((tq, HEAD_DIM), lambda b, h, i: (b * n_q + i, h)),
        out_shape=jax.ShapeDtypeStruct((batch * seq, heads * HEAD_DIM), BF16),
        scratch_shapes=[pltpu.VMEM((tk, tq), F32)] * 3 + [pltpu.VMEM((tk, tq), BF16)] * 3 + [
                        pltpu.VMEM((1, tq), F32)] * 6 + [
                        pltpu.VMEM((1, tq), F32), pltpu.VMEM((V_ROWS, tq), F32)],
        compiler_params=_params(3),
        name="attn_latent",
    )(q_t, k, k, v_t, v_t)


def _attn_ctx_kernel(q_t_ref, k_ref, v_t_ref, o_ref):
    s = jnp.dot(k_ref[...], q_t_ref[...], preferred_element_type=F32)
    p = jnp.exp2((s - jnp.max(s, axis=0, keepdims=True)).astype(BF16))
    o_t = jnp.dot(v_t_ref[0], p, preferred_element_type=F32)
    o_ref[...] = (o_t[:HEAD_DIM, :] / o_t[HEAD_DIM:HEAD_DIM + 1, :]).T.astype(o_ref.dtype)


def _attend_ctx(q_t, k, v_t, batch, seq, ctx_len, heads):
    ctx0 = batch * seq // ctx_len
    return pl.pallas_call(
        _attn_ctx_kernel,
        grid=(batch, heads),
        in_specs=[pl.BlockSpec((QK_PAD, ctx_len), lambda b, h: (h, ctx0 + b)),
                  pl.BlockSpec((ctx_len, QK_PAD), lambda b, h: (ctx0 + b, h)),
                  pl.BlockSpec((1, V_ROWS, ctx_len), lambda b, h: (ctx0 + b, h, 0))],
        out_specs=pl.BlockSpec((ctx_len, HEAD_DIM), lambda b, h: (b, h)),
        out_shape=jax.ShapeDtypeStruct((batch * ctx_len, heads * HEAD_DIM), BF16),
        compiler_params=_params(2),
        name="attn_ctx",
    )(q_t, k, v_t)


def _merge_kernel(u_ref, bg_ref, cg_ref, zc_ref, zm_ref, up_ref, cp_ref, un_ref, cn_ref,
                  ya_ref, cw_ref, yc_ref, ym_ref, *, tm, lat_rows, seq, ctx_len):
    i = pl.program_id(0)
    row0 = i * tm
    row1 = row0 + tm
    is_start = jnp.where(row0 < lat_rows, row0 % seq == 0, (row0 - lat_rows) % ctx_len == 0)
    is_end = jnp.where(row1 <= lat_rows, row1 % seq == 0, (row1 - lat_rows) % ctx_len == 0)
    v = cg_ref[...].astype(F32) * u_ref[...].astype(F32)
    v_before = (cp_ref[SUBLANE - 1:SUBLANE, :].astype(F32) * up_ref[SUBLANE - 1:SUBLANE, :].astype(F32))
    v_after = cn_ref[0:1, :].astype(F32) * un_ref[0:1, :].astype(F32)
    v_before = jnp.where(is_start, 0.0, v_before)
    v_after = jnp.where(is_end, 0.0, v_after)
    row = lax.broadcasted_iota(jnp.int32, v.shape, 0)
    v_prev = jnp.where(row == 0, v_before, pltpu.roll(v, 1, axis=0))
    v_next = jnp.where(row == tm - 1, v_after, pltpu.roll(v, tm - 1, axis=0))
    cw = cw_ref[...]
    y = v_prev * cw[0:1, :] + v * cw[1:2, :] + v_next * cw[2:3, :]
    yc_ref[...] = (bg_ref[...].astype(F32) * y * _silu(zc_ref[...].astype(F32))).astype(yc_ref.dtype)
    ym_ref[...] = (ya_ref[...].astype(F32) * _silu(zm_ref[...].astype(F32))).astype(ym_ref.dtype)


def _merge(p_conv, p_zm, y_attn, conv_w, layer, rows, lat_rows, seq, ctx_len, tm, tc):
    m_all = p_conv.shape[0]
    cw = conv_w.shape[-1]
    nc = cw // tc
    hb = tm // SUBLANE
    last_hb = m_all // SUBLANE - 1
    main = lambda s: pl.BlockSpec((tm, tc), lambda i, j: (i, s * nc + j))
    prev = lambda s: pl.BlockSpec((SUBLANE, tc), lambda i, j: (jnp.maximum(i * hb - 1, 0), s * nc + j))
    nxt = lambda s: pl.BlockSpec((SUBLANE, tc), lambda i, j: (jnp.minimum((i + 1) * hb, last_hb), s * nc + j))
    out = pl.BlockSpec((tm, tc), lambda i, j: (i, j))
    return pl.pallas_call(
        functools.partial(_merge_kernel, tm=tm, lat_rows=lat_rows, seq=seq, ctx_len=ctx_len),
        grid=(rows // tm, nc),
        in_specs=[main(0), main(1), main(2), main(3), out,
                  prev(0), prev(2), nxt(0), nxt(2),
                  out,
                  pl.BlockSpec((None, 3, tc), lambda i, j: (layer, 0, j))],
        out_specs=[out, out],
        out_shape=[jax.ShapeDtypeStruct((rows, cw), BF16), jax.ShapeDtypeStruct((rows, cw), BF16)],
        compiler_params=_params(2),
        name="conv_merge",
    )(p_conv, p_conv, p_conv, p_conv, p_zm, p_conv, p_conv, p_conv, p_conv, y_attn, conv_w)


def _layernorm(z, g, b):
    mu = jnp.mean(z, axis=-1, keepdims=True)
    zc = z - mu
    var = jnp.mean(zc * zc, axis=-1, keepdims=True)
    return zc * lax.rsqrt(var + EPS) * g + b


def _postln_next_kernel(x_ref, o_ref, gate_ref, g_ref, b_ref, sh_ref, sc_ref, xn_ref, h_ref, *, alpha):
    xn = _layernorm(alpha * x_ref[...] + gate_ref[...] * o_ref[...], g_ref[...], b_ref[...])
    xn_ref[...] = xn
    h_ref[...] = (xn * (1.0 + sc_ref[...]) + sh_ref[...]).astype(h_ref.dtype)


def _postln_last_kernel(x_ref, o_ref, gate_ref, g_ref, b_ref, xn_ref, *, alpha):
    xn_ref[...] = _layernorm(alpha * x_ref[...] + gate_ref[...] * o_ref[...], g_ref[...], b_ref[...])


def _postln(x_all, out, mods, ln_g, ln_b, layer, rows, group_of, tm, alpha, last):
    d = x_all.shape[1]
    depth = ln_g.shape[0]
    gof = lambda i: group_of(i, tm)
    tile = pl.BlockSpec((tm, d), lambda i: (i, 0))
    vec = pl.BlockSpec((None, 1, d), lambda i: (layer, 0, 0))
    in_specs = [tile, tile, _row_spec(mods, layer, 2, gof), vec, vec]
    args = [x_all, out, mods, ln_g.reshape(depth, 1, d), ln_b.reshape(depth, 1, d)]
    if last:
        return pl.pallas_call(
            functools.partial(_postln_last_kernel, alpha=alpha),
            grid=(rows // tm,), in_specs=in_specs, out_specs=tile,
            out_shape=jax.ShapeDtypeStruct((rows, d), F32),
            compiler_params=_params(1), name="postln_last",
        )(*args)
    in_specs += [_row_spec(mods, layer + 1, 0, gof), _row_spec(mods, layer + 1, 1, gof)]
    args += [mods, mods]
    return pl.pallas_call(
        functools.partial(_postln_next_kernel, alpha=alpha),
        grid=(rows // tm,), in_specs=in_specs, out_specs=[tile, tile],
        out_shape=[jax.ShapeDtypeStruct((rows, d), F32), jax.ShapeDtypeStruct((rows, d), BF16)],
        compiler_params=_params(1), name="postln_next",
    )(*args)


def _rot_cols(w):
    shp = w.shape
    wr = w.reshape(shp[:-1] + (2, 2, ROPE_FREQ))
    return jnp.stack([-wr[..., 1, :], wr[..., 0, :]], axis=-2).reshape(shp)


def _rope_tables(batch, seq, ctx_rows):
    rows = seq // GRID_W
    row = jnp.repeat(jnp.arange(rows, dtype=F32), GRID_W)
    col = jnp.tile(jnp.arange(GRID_W, dtype=F32), rows)
    pos = jnp.stack([row, col], axis=-1)
    inv_freq = ROPE_BASE ** (-(jnp.arange(ROPE_FREQ, dtype=F32) * 2.0) / (ROPE_DIM // 2))
    ang = pos[:, :, None] * inv_freq
    expand = lambda t: jnp.broadcast_to(t[:, :, None, :], (seq, 2, 2, ROPE_FREQ)).reshape(seq, ROPE_DIM)
    cos = jnp.concatenate([jnp.tile(expand(jnp.cos(ang)), (batch, 1)), jnp.ones((ctx_rows, ROPE_DIM), F32)])
    sin = jnp.concatenate([jnp.tile(expand(jnp.sin(ang)), (batch, 1)), jnp.zeros((ctx_rows, ROPE_DIM), F32)])
    pad = lambda t: jnp.pad(t, ((0, 0), (0, LANE - ROPE_DIM)))
    cos, sin = pad(cos), pad(sin)
    return cos, sin, cos.T, sin.T


def kernel(x, c, ctx, c_ctx, w_ada, b_ada, w_in, conv_w, q_norm_g, w_uq, kv_norm_g, w_ukv, w_out, ln_g, ln_b):
    batch, seq, d = x.shape
    ctx_len = ctx.shape[1]
    depth = w_ada.shape[0]
    cw = conv_w.shape[-1]
    heads = (w_out.shape[1] - cw) // HEAD_DIM
    ql, kvl = q_norm_g.shape[-1], kv_norm_g.shape[-1]
    lat_rows, ctx_rows = batch * seq, batch * ctx_len
    m_all = lat_rows + ctx_rows
    alpha = (2.0 * depth) ** 0.25

    sec = [cw, cw, cw, cw, ql, kvl, ROPE_DIM, cw]
    offs = [0]
    for s in sec:
        offs.append(offs[-1] + s)
    w_conv = w_in[:, :, :offs[4]].astype(BF16)
    w_tail = w_in[:, :, offs[4]:].astype(BF16)
    tail = lambda k: w_tail[:, :, offs[k] - offs[4]:offs[k + 1] - offs[4]]
    w_zm = tail(7)
    w_lora = jnp.concatenate([tail(4), tail(5), tail(6), _rot_cols(tail(6))], axis=-1)
    wq = w_uq.reshape(depth, ql, heads, HEAD_DIM + ROPE_DIM)
    wq = jnp.concatenate([wq, _rot_cols(wq[..., HEAD_DIM:])], axis=-1).reshape(depth, ql, heads * QK_PAD)
    wq_t = jnp.swapaxes(wq, 1, 2).astype(BF16)
    wkv = w_ukv.reshape(depth, kvl, heads, 2 * HEAD_DIM)
    wk = wkv[..., :HEAD_DIM].reshape(depth, kvl, heads * HEAD_DIM).astype(BF16)
    wv_t = jnp.swapaxes(wkv[..., HEAD_DIM:].reshape(depth, kvl, heads * HEAD_DIM), 1, 2).astype(BF16)
    wo_conv = w_out[:, :cw, :].astype(BF16)
    wo_mla = w_out[:, cw:, :].astype(BF16)
    tables = _rope_tables(batch, seq, ctx_rows)

    cc = jnp.zeros((SUBLANE, d), F32).at[:batch].set(c).at[batch].set(c_ctx)
    mods = _modulation(cc, w_ada, b_ada)[:, :batch + 1].reshape(depth, batch + 1, 3, 1, d)
    group_of = lambda i, tm: jnp.minimum((i * tm) // seq, batch)

    tm = _tile(math.gcd(seq, ctx_rows), (512, 256, 128))
    tm_small = _tile(math.gcd(seq, ctx_len), (256, 128))
    assert tm_small == ctx_len, "the context keys must form exactly one key chunk"
    x_all = jnp.concatenate([x.reshape(lat_rows, d), ctx.reshape(ctx_rows, d)], axis=0)
    h = _modulate(x_all, mods, 0, group_of, tm)

    for layer in range(depth):
        last = layer == depth - 1
        rows = lat_rows if last else m_all
        tn = _tile(cw, (1024, 512, 256, 128))
        p_conv = _matmul(h, w_conv, layer, BF16, rows, tm, tn, "in_proj_conv")
        p_zm = _matmul(h, w_zm, layer, BF16, rows, tm, tn, "in_proj_zm")
        p_lora = _matmul(h, w_lora, layer, F32, m_all, tm, w_lora.shape[-1], "in_proj_lora")
        q_t, k, v_t = _up_project(p_lora, tables, q_norm_g, kv_norm_g, wq_t, wk, wv_t, layer, heads, tm_small)
        y_attn = _attend_latent(q_t, k, v_t, batch, seq, ctx_len, heads, _tile(seq, (2048, 1024, 512, 256, 128)), tm_small)
        if not last:
            y_attn = jnp.concatenate([y_attn, _attend_ctx(q_t, k, v_t, batch, seq, ctx_len, heads)], axis=0)
        y_conv, y_mla = _merge(p_conv, p_zm, y_attn, conv_w, layer, rows, lat_rows, seq, ctx_len,
                               tm_small, _tile(cw, (512, 256, 128)))
        out = _matmul2(y_conv, y_mla, wo_conv, wo_mla, layer, rows, tm, _tile(d, (1024, 512, 256, 128)), "out_proj")
        if last:
            x_all = _postln(x_all, out, mods, ln_g, ln_b, layer, rows, group_of, tm_small, alpha, True)
        else:
            x_all, h = _postln(x_all, out, mods, ln_g, ln_b, layer, rows, group_of, tm_small, alpha, False)
    return x_all.reshape(batch, seq, d)
```

```python
import functools
import math

import jax
import jax.numpy as jnp
from jax import lax
from jax.experimental import pallas as pl
from jax.experimental.pallas import tpu as pltpu

F32 = jnp.float32
BF16 = jnp.bfloat16

HEAD_DIM = 128
ROPE_DIM = 64
ROPE_FREQ = ROPE_DIM // 4
ROPE_BASE = 10000.0
GRID_W = 64
QK_PAD = 2 * HEAD_DIM
V_ROWS = HEAD_DIM + 16
EPS = 1e-6
ATTN_SCALE = 1.0 / math.sqrt(HEAD_DIM + ROPE_DIM)
Q_PRESCALE = ATTN_SCALE * math.log2(math.e)
ATTN_UNROLL = 9
LANE = 128
SUBLANE = 8
VMEM_LIMIT = 56 * 1024 * 1024


def _params(n_axes, vmem=VMEM_LIMIT):
    return pltpu.CompilerParams(dimension_semantics=("arbitrary",) * n_axes,
                                vmem_limit_bytes=vmem)


def _tile(n, prefs):
    for t in prefs:
        if n % t == 0:
            return t
    return n


def _silu(x):
    h = 0.5 * x
    return h + h * jnp.tanh(h)


def _mod_kernel(a_ref, w_ref, b_ref, o_ref):
    a = _silu(a_ref[...]).astype(BF16)
    o_ref[...] = jnp.dot(a, w_ref[...].astype(BF16), preferred_element_type=F32) + b_ref[...]


def _modulation(cc, w_ada, b_ada):
    depth, d, n = w_ada.shape
    tn = _tile(n, (512, 256, 128))
    return pl.pallas_call(
        _mod_kernel,
        grid=(depth, n // tn),
        in_specs=[pl.BlockSpec((SUBLANE, d), lambda l, j: (0, 0)),
                  pl.BlockSpec((None, d, tn), lambda l, j: (l, 0, j)),
                  pl.BlockSpec((None, 1, tn), lambda l, j: (l, 0, j))],
        out_specs=pl.BlockSpec((None, SUBLANE, tn), lambda l, j: (l, 0, j)),
        out_shape=jax.ShapeDtypeStruct((depth, SUBLANE, n), F32),
        compiler_params=_params(2),
        name="adaln_mod",
    )(cc, w_ada, b_ada.reshape(depth, 1, n))


def _row_spec(mods, layer, part, group_of_tile):
    d = mods.shape[-1]
    return pl.BlockSpec((None, None, None, 1, d),
                        lambda i: (layer, group_of_tile(i), part, 0, 0))


def _modulate_kernel(x_ref, sh_ref, sc_ref, h_ref):
    h_ref[...] = (x_ref[...] * (1.0 + sc_ref[...]) + sh_ref[...]).astype(h_ref.dtype)


def _modulate(x_all, mods, layer, group_of, tm):
    m, d = x_all.shape
    gof = lambda i: group_of(i, tm)
    return pl.pallas_call(
        _modulate_kernel,
        grid=(m // tm,),
        in_specs=[pl.BlockSpec((tm, d), lambda i: (i, 0)),
                  _row_spec(mods, layer, 0, gof),
                  _row_spec(mods, layer, 1, gof)],
        out_specs=pl.BlockSpec((tm, d), lambda i: (i, 0)),
        out_shape=jax.ShapeDtypeStruct((m, d), BF16),
        compiler_params=_params(1),
        name="modulate",
    )(x_all, mods, mods)


def _mm_kernel(a_ref, b_ref, o_ref):
    o_ref[...] = jnp.dot(a_ref[...], b_ref[...], preferred_element_type=F32).astype(o_ref.dtype)


def _matmul(a, w, layer, out_dtype, rows, tm, tn, name, n_cols=None):
    k = a.shape[1]
    n = n_cols or w.shape[2]
    return pl.pallas_call(
        _mm_kernel,
        grid=(n // tn, rows // tm),
        in_specs=[pl.BlockSpec((tm, k), lambda j, i: (i, 0)),
                  pl.BlockSpec((None, k, tn), lambda j, i: (layer, 0, j))],
        out_specs=pl.BlockSpec((tm, tn), lambda j, i: (i, j)),
        out_shape=jax.ShapeDtypeStruct((rows, n), out_dtype),
        compiler_params=_params(2),
        name=name,
    )(a, w)


def _mm2_kernel(a1_ref, a2_ref, b1_ref, b2_ref, o_ref):
    acc = jnp.dot(a1_ref[...], b1_ref[...], preferred_element_type=F32)
    acc += jnp.dot(a2_ref[...], b2_ref[...], preferred_element_type=F32)
    o_ref[...] = acc.astype(o_ref.dtype)


def _matmul2(a1, a2, w1, w2, layer, rows, tm, tn, name):
    k = a1.shape[1]
    n = w1.shape[2]
    return pl.pallas_call(
        _mm2_kernel,
        grid=(n // tn, rows // tm),
        in_specs=[pl.BlockSpec((tm, k), lambda j, i: (i, 0)),
                  pl.BlockSpec((tm, k), lambda j, i: (i, 0)),
                  pl.BlockSpec((None, k, tn), lambda j, i: (layer, 0, j)),
                  pl.BlockSpec((None, k, tn), lambda j, i: (layer, 0, j))],
        out_specs=pl.BlockSpec((tm, tn), lambda j, i: (i, j)),
        out_shape=jax.ShapeDtypeStruct((rows, n), F32),
        compiler_params=_params(2),
        name=name,
    )(a1, a2, w1, w2)


def _rope_half(x, cos, sin):
    return x * cos + pltpu.roll(x, ROPE_DIM, axis=1) * sin


def _rms(x, g):
    return (x * lax.rsqrt(jnp.mean(x * x, axis=-1, keepdims=True) + EPS)) * g


def _dot_nt(a, b):
    return lax.dot_general(a, b, (((1,), (1,)), ((), ())), preferred_element_type=F32)


def _up_kernel(p_ref, cos_ref, sin_ref, cos_t_ref, sin_t_ref, gq_ref, gkv_ref, wq_t_ref, wk_ref, wv_t_ref,
               q_t_ref, k_ref, v_t_ref, *, ql, kvl, heads):
    cq = _rms(p_ref[:, :ql], gq_ref[...]).astype(BF16)
    ckv = _rms(p_ref[:, ql:ql + kvl], gkv_ref[...]).astype(BF16)
    krp = _rope_half(p_ref[:, ql + kvl:], cos_ref[...], sin_ref[...]).astype(BF16)
    cos_t = cos_t_ref[...] * Q_PRESCALE
    sin_t = sin_t_ref[...] * Q_PRESCALE
    q_t = _dot_nt(wq_t_ref[...], cq)
    kn = jnp.dot(ckv, wk_ref[...], preferred_element_type=F32)
    for h in range(heads):
        lo = h * QK_PAD
        hi = q_t[lo + HEAD_DIM:lo + QK_PAD, :]
        q_t_ref[lo:lo + HEAD_DIM, :] = (q_t[lo:lo + HEAD_DIM, :] * Q_PRESCALE).astype(BF16)
        q_t_ref[lo + HEAD_DIM:lo + QK_PAD, :] = (
            hi * cos_t + pltpu.roll(hi, ROPE_DIM, axis=0) * sin_t).astype(BF16)
        k_ref[:, lo:lo + HEAD_DIM] = kn[:, h * HEAD_DIM:(h + 1) * HEAD_DIM].astype(BF16)
        k_ref[:, lo + HEAD_DIM:lo + QK_PAD] = krp
    v_t = _dot_nt(wv_t_ref[...], ckv)
    for h in range(heads):
        lo = h * V_ROWS
        v_t_ref[lo:lo + HEAD_DIM, :] = v_t[h * HEAD_DIM:(h + 1) * HEAD_DIM, :].astype(BF16)
        v_t_ref[lo + HEAD_DIM:lo + V_ROWS, :] = jnp.ones((V_ROWS - HEAD_DIM, v_t.shape[1]), BF16)


def _up_project(p_lora, tables, gq, gkv, wq_t, wk, wv_t, layer, heads, tm):
    m, pw = p_lora.shape
    ql, kvl = gq.shape[-1], gkv.shape[-1]
    depth = wq_t.shape[0]
    cos_r, sin_r, cos_t, sin_t = tables
    const = lambda i: (layer, 0, 0)
    return pl.pallas_call(
        functools.partial(_up_kernel, ql=ql, kvl=kvl, heads=heads),
        grid=(m // tm,),
        in_specs=[pl.BlockSpec((tm, pw), lambda i: (i, 0)),
                  pl.BlockSpec((tm, LANE), lambda i: (i, 0)),
                  pl.BlockSpec((tm, LANE), lambda i: (i, 0)),
                  pl.BlockSpec((LANE, tm), lambda i: (0, i)),
                  pl.BlockSpec((LANE, tm), lambda i: (0, i)),
                  pl.BlockSpec((None, 1, ql), const),
                  pl.BlockSpec((None, 1, kvl), const),
                  pl.BlockSpec((None, heads * QK_PAD, ql), const),
                  pl.BlockSpec((None, kvl, heads * HEAD_DIM), const),
                  pl.BlockSpec((None, heads * HEAD_DIM, kvl), const)],
        out_specs=[pl.BlockSpec((heads * QK_PAD, tm), lambda i: (0, i)),
                   pl.BlockSpec((tm, heads * QK_PAD), lambda i: (i, 0)),
                   pl.BlockSpec((None, heads * V_ROWS, tm), lambda i: (i, 0, 0))],
        out_shape=[jax.ShapeDtypeStruct((heads * QK_PAD, m), BF16),
                   jax.ShapeDtypeStruct((m, heads * QK_PAD), BF16),
                   jax.ShapeDtypeStruct((m // tm, heads * V_ROWS, tm), BF16)],
        compiler_params=_params(1),
        name="lora_up",
    )(p_lora, cos_r, sin_r, cos_t, sin_t, gq.reshape(depth, 1, ql), gkv.reshape(depth, 1, kvl), wq_t, wk, wv_t)


def _attn_kernel(q_t_ref, kl_ref, kc_ref, vl_ref, vc_ref, o_ref,
                 s0_ref, s1_ref, s2_ref, p0_ref, p1_ref, p2_ref, a0_ref, a1_ref, a2_ref,
                 c0_ref, c1_ref, c2_ref, m_ref, acc_ref, *, tk, unroll):
    n_lat = kl_ref.shape[0] // tk
    n = n_lat + 1
    s_buf, p_buf, a_buf = (s0_ref, s1_ref, s2_ref), (p0_ref, p1_ref, p2_ref), (a0_ref, a1_ref, a2_ref)
    c_buf = (c0_ref, c1_ref, c2_ref)
    q_t = q_t_ref[...]

    def k_chunk(c):
        if isinstance(c, int) and c == n_lat:
            return kc_ref[...]
        return kl_ref[pl.ds(pl.multiple_of(c * tk, tk), tk), :]

    def v_chunk(c):
        if isinstance(c, int) and c == n_lat:
            return vc_ref[0]
        return vl_ref[c]

    def scores(c, slot):
        s = jnp.dot(k_chunk(c), q_t, preferred_element_type=F32)
        s_buf[slot][...] = s
        c_buf[slot][...] = jnp.max(s, axis=0, keepdims=True)

    def softmax(slot):
        m_old = m_ref[...]
        m_new = jnp.maximum(m_old, c_buf[slot][...])
        alpha = jnp.exp2(m_old - m_new)
        m_ref[...] = m_new
        a_buf[slot][...] = alpha
        p_buf[slot][...] = jnp.exp2((s_buf[slot][...] - m_new).astype(BF16))

    def values(c, slot):
        acc_ref[...] = a_buf[slot][...] * acc_ref[...] + jnp.dot(
            v_chunk(c), p_buf[slot][...], preferred_element_type=F32)

    def stage(c, slot, with_scores=True, with_values=True):
        if with_scores:
            scores(c + 2, (slot + 2) % 3)
        if with_values:
            values(c - 2, (slot + 1) % 3)
        softmax(slot)

    m_ref[...] = jnp.full(m_ref.shape, -jnp.inf, F32)
    acc_ref[...] = jnp.zeros(acc_ref.shape, F32)
    scores(0, 0)
    scores(1, 1)
    for c in range(2):
        stage(c, c, with_scores=c + 2 < n, with_values=False)
    n_trips = max(n - 5, 0) // unroll

    def trip(j, carry):
        for r in range(unroll):
            stage(2 + unroll * j + r, (2 + r) % 3)
        return carry

    lax.fori_loop(0, n_trips, trip, 0)
    for c in range(2 + unroll * n_trips, n):
        stage(c, c % 3, with_scores=c + 2 < n)
    values(n - 2, (n - 2) % 3)
    values(n - 1, (n - 1) % 3)
    o_t = acc_ref[:HEAD_DIM, :] / acc_ref[HEAD_DIM:HEAD_DIM + 1, :]
    o_ref[...] = o_t.T.astype(o_ref.dtype)


def _attend_latent(q_t, k, v_t, batch, seq, ctx_len, heads, tq, tk):
    n_q = seq // tq
    ctx0 = batch * seq // ctx_len
    n_lat = seq // tk
    return pl.pallas_call(
        functools.partial(_attn_kernel, tk=tk, unroll=ATTN_UNROLL),
        grid=(batch, heads, n_q),
        in_specs=[pl.BlockSpec((QK_PAD, tq), lambda b, h, i: (h, b * n_q + i)),
                  pl.BlockSpec((seq, QK_PAD), lambda b, h, i: (b, h)),
                  pl.BlockSpec((ctx_len, QK_PAD), lambda b, h, i: (ctx0 + b, h)),
                  pl.BlockSpec((n_lat, V_ROWS, tk), lambda b, h, i: (b, h, 0)),
                  pl.BlockSpec((ctx_len // tk, V_ROWS, tk), lambda b, h, i: (ctx0 + b, h, 0))],
        out_specs=pl.BlockSpec((tq, HEAD_DIM), lambda b, h, i: (b * n_q + i, h)),
        out_shape=jax.ShapeDtypeStruct((batch * seq, heads * HEAD_DIM), BF16),
        scratch_shapes=[pltpu.VMEM((tk, tq), F32)] * 3 + [pltpu.VMEM((tk, tq), BF16)] * 3 + [
                        pltpu.VMEM((1, tq), F32)] * 6 + [
                        pltpu.VMEM((1, tq), F32), pltpu.VMEM((V_ROWS, tq), F32)],
        compiler_params=_params(3),
        name="attn_latent",
    )(q_t, k, k, v_t, v_t)


def _attn_ctx_kernel(q_t_ref, k_ref, v_t_ref, o_ref):
    s = jnp.dot(k_ref[...], q_t_ref[...], preferred_element_type=F32)
    p = jnp.exp2((s - jnp.max(s, axis=0, keepdims=True)).astype(BF16))
    o_t = jnp.dot(v_t_ref[0], p, preferred_element_type=F32)
    o_ref[...] = (o_t[:HEAD_DIM, :] / o_t[HEAD_DIM:HEAD_DIM + 1, :]).T.astype(o_ref.dtype)


def _attend_ctx(q_t, k, v_t, batch, seq, ctx_len, heads):
    ctx0 = batch * seq // ctx_len
    return pl.pallas_call(
        _attn_ctx_kernel,
        grid=(batch, heads),
        in_specs=[pl.BlockSpec((QK_PAD, ctx_len), lambda b, h: (h, ctx0 + b)),
                  pl.BlockSpec((ctx_len, QK_PAD), lambda b, h: (ctx0 + b, h)),
                  pl.BlockSpec((1, V_ROWS, ctx_len), lambda b, h: (ctx0 + b, h, 0))],
        out_specs=pl.BlockSpec((ctx_len, HEAD_DIM), lambda b, h: (b, h)),
        out_shape=jax.ShapeDtypeStruct((batch * ctx_len, heads * HEAD_DIM), BF16),
        compiler_params=_params(2),
        name="attn_ctx",
    )(q_t, k, v_t)


def _merge_kernel(u_ref, bg_ref, cg_ref, zc_ref, zm_ref, up_ref, cp_ref, un_ref, cn_ref,
                  ya_ref, cw_ref, yc_ref, ym_ref, *, tm, lat_rows, seq, ctx_len):
    i = pl.program_id(0)
    row0 = i * tm
    row1 = row0 + tm
    is_start = jnp.where(row0 < lat_rows, row0 % seq == 0, (row0 - lat_rows) % ctx_len == 0)
    is_end = jnp.where(row1 <= lat_rows, row1 % seq == 0, (row1 - lat_rows) % ctx_len == 0)
    v = (cg_ref[...] * u_ref[...]).astype(F32)
    v_before = (cp_ref[SUBLANE - 1:SUBLANE, :] * up_ref[SUBLANE - 1:SUBLANE, :]).astype(F32)
    v_after = (cn_ref[0:1, :] * un_ref[0:1, :]).astype(F32)
    v_before = jnp.where(is_start, 0.0, v_before)
    v_after = jnp.where(is_end, 0.0, v_after)
    row = lax.broadcasted_iota(jnp.int32, v.shape, 0)
    v_prev = jnp.where(row == 0, v_before, pltpu.roll(v, 1, axis=0))
    v_next = jnp.where(row == tm - 1, v_after, pltpu.roll(v, tm - 1, axis=0))
    cw = cw_ref[...]
    y = (v_prev * cw[0:1, :] + v * cw[1:2, :] + v_next * cw[2:3, :]).astype(BF16)
    yc_ref[...] = bg_ref[...] * y * _silu(zc_ref[...])
    ym_ref[...] = ya_ref[...] * _silu(zm_ref[...])


def _merge(p_conv, p_zm, y_attn, conv_w, layer, rows, lat_rows, seq, ctx_len, tm, tc):
    m_all = p_conv.shape[0]
    cw = conv_w.shape[-1]
    nc = cw // tc
    hb = tm // SUBLANE
    last_hb = m_all // SUBLANE - 1
    main = lambda s: pl.BlockSpec((tm, tc), lambda i, j: (i, s * nc + j))
    prev = lambda s: pl.BlockSpec((SUBLANE, tc), lambda i, j: (jnp.maximum(i * hb - 1, 0), s * nc + j))
    nxt = lambda s: pl.BlockSpec((SUBLANE, tc), lambda i, j: (jnp.minimum((i + 1) * hb, last_hb), s * nc + j))
    out = pl.BlockSpec((tm, tc), lambda i, j: (i, j))
    return pl.pallas_call(
        functools.partial(_merge_kernel, tm=tm, lat_rows=lat_rows, seq=seq, ctx_len=ctx_len),
        grid=(rows // tm, nc),
        in_specs=[main(0), main(1), main(2), main(3), out,
                  prev(0), prev(2), nxt(0), nxt(2),
                  out,
                  pl.BlockSpec((None, 3, tc), lambda i, j: (layer, 0, j))],
        out_specs=[out, out],
        out_shape=[jax.ShapeDtypeStruct((rows, cw), BF16), jax.ShapeDtypeStruct((rows, cw), BF16)],
        compiler_params=_params(2),
        name="conv_merge",
    )(p_conv, p_conv, p_conv, p_conv, p_zm, p_conv, p_conv, p_conv, p_conv, y_attn, conv_w)


def _layernorm(z, g, b):
    mu = jnp.mean(z, axis=-1, keepdims=True)
    zc = z - mu
    var = jnp.mean(zc * zc, axis=-1, keepdims=True)
    return zc * lax.rsqrt(var + EPS) * g + b


def _postln_next_kernel(x_ref, o_ref, gate_ref, g_ref, b_ref, sh_ref, sc_ref, xn_ref, h_ref, *, alpha):
    xn = _layernorm(alpha * x_ref[...] + gate_ref[...] * o_ref[...], g_ref[...], b_ref[...])
    xn_ref[...] = xn
    h_ref[...] = (xn * (1.0 + sc_ref[...]) + sh_ref[...]).astype(h_ref.dtype)


def _postln_last_kernel(x_ref, o_ref, gate_ref, g_ref, b_ref, xn_ref, *, alpha):
    xn_ref[...] = _layernorm(alpha * x_ref[...] + gate_ref[...] * o_ref[...], g_ref[...], b_ref[...])


def _postln(x_all, out, mods, ln_g, ln_b, layer, rows, group_of, tm, alpha, last):
    d = x_all.shape[1]
    depth = ln_g.shape[0]
    gof = lambda i: group_of(i, tm)
    tile = pl.BlockSpec((tm, d), lambda i: (i, 0))
    vec = pl.BlockSpec((None, 1, d), lambda i: (layer, 0, 0))
    in_specs = [tile, tile, _row_spec(mods, layer, 2, gof), vec, vec]
    args = [x_all, out, mods, ln_g.reshape(depth, 1, d), ln_b.reshape(depth, 1, d)]
    if last:
        return pl.pallas_call(
            functools.partial(_postln_last_kernel, alpha=alpha),
            grid=(rows // tm,), in_specs=in_specs, out_specs=tile,
            out_shape=jax.ShapeDtypeStruct((rows, d), F32),
            compiler_params=_params(1), name="postln_last",
        )(*args)
    in_specs += [_row_spec(mods, layer + 1, 0, gof), _row_spec(mods, layer + 1, 1, gof)]
    args += [mods, mods]
    return pl.pallas_call(
        functools.partial(_postln_next_kernel, alpha=alpha),
        grid=(rows // tm,), in_specs=in_specs, out_specs=[tile, tile],
        out_shape=[jax.ShapeDtypeStruct((rows, d), F32), jax.ShapeDtypeStruct((rows, d), BF16)],
        compiler_params=_params(1), name="postln_next",
    )(*args)


def _rot_cols(w):
    shp = w.shape
    wr = w.reshape(shp[:-1] + (2, 2, ROPE_FREQ))
    return jnp.stack([-wr[..., 1, :], wr[..., 0, :]], axis=-2).reshape(shp)


def _rope_tables(batch, seq, ctx_rows):
    rows = seq // GRID_W
    row = jnp.repeat(jnp.arange(rows, dtype=F32), GRID_W)
    col = jnp.tile(jnp.arange(GRID_W, dtype=F32), rows)
    pos = jnp.stack([row, col], axis=-1)
    inv_freq = ROPE_BASE ** (-(jnp.arange(ROPE_FREQ, dtype=F32) * 2.0) / (ROPE_DIM // 2))
    ang = pos[:, :, None] * inv_freq
    expand = lambda t: jnp.broadcast_to(t[:, :, None, :], (seq, 2, 2, ROPE_FREQ)).reshape(seq, ROPE_DIM)
    cos = jnp.concatenate([jnp.tile(expand(jnp.cos(ang)), (batch, 1)), jnp.ones((ctx_rows, ROPE_DIM), F32)])
    sin = jnp.concatenate([jnp.tile(expand(jnp.sin(ang)), (batch, 1)), jnp.zeros((ctx_rows, ROPE_DIM), F32)])
    pad = lambda t: jnp.pad(t, ((0, 0), (0, LANE - ROPE_DIM)))
    cos, sin = pad(cos), pad(sin)
    return cos, sin, cos.T, sin.T


def kernel(x, c, ctx, c_ctx, w_ada, b_ada, w_in, conv_w, q_norm_g, w_uq, kv_norm_g, w_ukv, w_out, ln_g, ln_b):
    batch, seq, d = x.shape
    ctx_len = ctx.shape[1]
    depth = w_ada.shape[0]
    cw = conv_w.shape[-1]
    heads = (w_out.shape[1] - cw) // HEAD_DIM
    ql, kvl = q_norm_g.shape[-1], kv_norm_g.shape[-1]
    lat_rows, ctx_rows = batch * seq, batch * ctx_len
    m_all = lat_rows + ctx_rows
    alpha = (2.0 * depth) ** 0.25

    sec = [cw, cw, cw, cw, ql, kvl, ROPE_DIM, cw]
    offs = [0]
    for s in sec:
        offs.append(offs[-1] + s)
    w_in_bf = w_in.astype(BF16)
    col = lambda k: w_in_bf[:, :, offs[k]:offs[k + 1]]
    w_zm = col(7)
    w_lora = jnp.concatenate([w_in_bf[:, :, offs[4]:offs[7]], _rot_cols(col(6))], axis=-1)
    wq = w_uq.reshape(depth, ql, heads, HEAD_DIM + ROPE_DIM)
    wq = jnp.concatenate([wq, _rot_cols(wq[..., HEAD_DIM:])], axis=-1).reshape(depth, ql, heads * QK_PAD)
    wq_t = jnp.swapaxes(wq, 1, 2).astype(BF16)
    wkv = w_ukv.reshape(depth, kvl, heads, 2 * HEAD_DIM)
    wk = wkv[..., :HEAD_DIM].reshape(depth, kvl, heads * HEAD_DIM).astype(BF16)
    wv_t = jnp.swapaxes(wkv[..., HEAD_DIM:].reshape(depth, kvl, heads * HEAD_DIM), 1, 2).astype(BF16)
    wo_conv = w_out[:, :cw, :].astype(BF16)
    wo_mla = w_out[:, cw:, :].astype(BF16)
    tables = _rope_tables(batch, seq, ctx_rows)

    cc = jnp.zeros((SUBLANE, d), F32).at[:batch].set(c).at[batch].set(c_ctx)
    mods = _modulation(cc, w_ada, b_ada)[:, :batch + 1].reshape(depth, batch + 1, 3, 1, d)
    group_of = lambda i, tm: jnp.minimum((i * tm) // seq, batch)

    tm = _tile(math.gcd(seq, ctx_rows), (512, 256, 128))
    tm_small = _tile(math.gcd(seq, ctx_len), (256, 128))
    assert tm_small == ctx_len, "the context keys must form exactly one key chunk"
    x_all = jnp.concatenate([x.reshape(lat_rows, d), ctx.reshape(ctx_rows, d)], axis=0)
    h = _modulate(x_all, mods, 0, group_of, tm)

    for layer in range(depth):
        last = layer == depth - 1
        rows = lat_rows if last else m_all
        tn = _tile(cw, (1024, 512, 256, 128))
        p_conv = _matmul(h, w_in_bf, layer, BF16, rows, tm, tn, "in_proj_conv", n_cols=4 * cw)
        p_zm = _matmul(h, w_zm, layer, BF16, rows, tm, tn, "in_proj_zm")
        p_lora = _matmul(h, w_lora, layer, F32, m_all, tm, w_lora.shape[-1], "in_proj_lora")
        q_t, k, v_t = _up_project(p_lora, tables, q_norm_g, kv_norm_g, wq_t, wk, wv_t, layer, heads, tm_small)
        y_attn = _attend_latent(q_t, k, v_t, batch, seq, ctx_len, heads, _tile(seq, (2048, 1024, 512, 256, 128)), tm_small)
        if not last:
            y_attn = jnp.concatenate([y_attn, _attend_ctx(q_t, k, v_t, batch, seq, ctx_len, heads)], axis=0)
        y_conv, y_mla = _merge(p_conv, p_zm, y_attn, conv_w, layer, rows, lat_rows, seq, ctx_len,
                               tm_small, _tile(cw, (1024, 512, 256, 128)))
        out = _matmul2(y_conv, y_mla, wo_conv, wo_mla, layer, rows, tm, _tile(d, (1024, 512, 256, 128)), "out_proj")
        if last:
            x_all = _postln(x_all, out, mods, ln_g, ln_b, layer, rows, group_of, tm_small, alpha, True)
        else:
            x_all, h = _postln(x_all, out, mods, ln_g, ln_b, layer, rows, group_of, tm_small, alpha, False)
    return x_all.reshape(batch, seq, d)
```

```python
import functools
import math

import jax
import jax.numpy as jnp
from jax import lax
from jax.experimental import pallas as pl
from jax.experimental.pallas import tpu as pltpu

F32 = jnp.float32
BF16 = jnp.bfloat16

HEAD_DIM = 128
ROPE_DIM = 64
ROPE_FREQ = ROPE_DIM // 4
ROPE_BASE = 10000.0
GRID_W = 64
QK_PAD = 2 * HEAD_DIM
V_ROWS = HEAD_DIM + 16
EPS = 1e-6
ATTN_SCALE = 1.0 / math.sqrt(HEAD_DIM + ROPE_DIM)
Q_PRESCALE = ATTN_SCALE * math.log2(math.e)
ATTN_UNROLL = 9
LANE = 128
SUBLANE = 8
VMEM_LIMIT = 56 * 1024 * 1024


def _params(n_axes, vmem=VMEM_LIMIT):
    return pltpu.CompilerParams(dimension_semantics=("arbitrary",) * n_axes,
                                vmem_limit_bytes=vmem)


def _tile(n, prefs):
    for t in prefs:
        if n % t == 0:
            return t
    return n


def _silu(x):
    h = 0.5 * x
    return h + h * jnp.tanh(h)


def _mod_kernel(a_ref, w_ref, b_ref, o_ref):
    a = _silu(a_ref[...]).astype(BF16)
    o_ref[...] = jnp.dot(a, w_ref[...].astype(BF16), preferred_element_type=F32) + b_ref[...]


def _modulation(cc, w_ada, b_ada):
    depth, d, n = w_ada.shape
    tn = _tile(n, (512, 256, 128))
    return pl.pallas_call(
        _mod_kernel,
        grid=(depth, n // tn),
        in_specs=[pl.BlockSpec((SUBLANE, d), lambda l, j: (0, 0)),
                  pl.BlockSpec((None, d, tn), lambda l, j: (l, 0, j)),
                  pl.BlockSpec((None, 1, tn), lambda l, j: (l, 0, j))],
        out_specs=pl.BlockSpec((None, SUBLANE, tn), lambda l, j: (l, 0, j)),
        out_shape=jax.ShapeDtypeStruct((depth, SUBLANE, n), F32),
        compiler_params=_params(2),
        name="adaln_mod",
    )(cc, w_ada, b_ada.reshape(depth, 1, n))


def _row_spec(mods, layer, part, group_of_tile):
    d = mods.shape[-1]
    return pl.BlockSpec((None, None, None, 1, d),
                        lambda i: (layer, group_of_tile(i), part, 0, 0))


def _modulate_kernel(x_ref, sh_ref, sc_ref, h_ref):
    h_ref[...] = (x_ref[...] * (1.0 + sc_ref[...]) + sh_ref[...]).astype(h_ref.dtype)


def _modulate(x_all, mods, layer, group_of, tm):
    m, d = x_all.shape
    gof = lambda i: group_of(i, tm)
    return pl.pallas_call(
        _modulate_kernel,
        grid=(m // tm,),
        in_specs=[pl.BlockSpec((tm, d), lambda i: (i, 0)),
                  _row_spec(mods, layer, 0, gof),
                  _row_spec(mods, layer, 1, gof)],
        out_specs=pl.BlockSpec((tm, d), lambda i: (i, 0)),
        out_shape=jax.ShapeDtypeStruct((m, d), BF16),
        compiler_params=_params(1),
        name="modulate",
    )(x_all, mods, mods)


def _mm_kernel(a_ref, b_ref, o_ref):
    o_ref[...] = jnp.dot(a_ref[...], b_ref[...], preferred_element_type=F32).astype(o_ref.dtype)


def _matmul(a, w, layer, out_dtype, rows, tm, tn, name, n_cols=None):
    k = a.shape[1]
    n = n_cols or w.shape[2]
    return pl.pallas_call(
        _mm_kernel,
        grid=(n // tn, rows // tm),
        in_specs=[pl.BlockSpec((tm, k), lambda j, i: (i, 0)),
                  pl.BlockSpec((None, k, tn), lambda j, i: (layer, 0, j))],
        out_specs=pl.BlockSpec((tm, tn), lambda j, i: (i, j)),
        out_shape=jax.ShapeDtypeStruct((rows, n), out_dtype),
        compiler_params=_params(2),
        name=name,
    )(a, w)


def _mm2_kernel(a1_ref, a2_ref, b1_ref, b2_ref, o_ref):
    acc = jnp.dot(a1_ref[...], b1_ref[...], preferred_element_type=F32)
    acc += jnp.dot(a2_ref[...], b2_ref[...], preferred_element_type=F32)
    o_ref[...] = acc.astype(o_ref.dtype)


def _matmul2(a1, a2, w1, w2, layer, rows, tm, tn, name):
    k = a1.shape[1]
    n = w1.shape[2]
    return pl.pallas_call(
        _mm2_kernel,
        grid=(n // tn, rows // tm),
        in_specs=[pl.BlockSpec((tm, k), lambda j, i: (i, 0)),
                  pl.BlockSpec((tm, k), lambda j, i: (i, 0)),
                  pl.BlockSpec((None, k, tn), lambda j, i: (layer, 0, j)),
                  pl.BlockSpec((None, k, tn), lambda j, i: (layer, 0, j))],
        out_specs=pl.BlockSpec((tm, tn), lambda j, i: (i, j)),
        out_shape=jax.ShapeDtypeStruct((rows, n), F32),
        compiler_params=_params(2),
        name=name,
    )(a1, a2, w1, w2)


def _rope_half(x, cos, sin):
    return x * cos + pltpu.roll(x, ROPE_DIM, axis=1) * sin


def _rms(x, g):
    return (x * lax.rsqrt(jnp.mean(x * x, axis=-1, keepdims=True) + EPS)) * g


def _dot_nt(a, b):
    return lax.dot_general(a, b, (((1,), (1,)), ((), ())), preferred_element_type=F32)


def _up_kernel(p_ref, cos_ref, sin_ref, cos_t_ref, sin_t_ref, gq_ref, gkv_ref, wq_t_ref, wk_ref, wv_t_ref,
               q_t_ref, k_ref, v_t_ref, *, ql, kvl, heads):
    cq = _rms(p_ref[:, :ql], gq_ref[...]).astype(BF16)
    ckv = _rms(p_ref[:, ql:ql + kvl], gkv_ref[...]).astype(BF16)
    krp = _rope_half(p_ref[:, ql + kvl:], cos_ref[...], sin_ref[...]).astype(BF16)
    cos_t = cos_t_ref[...] * Q_PRESCALE
    sin_t = sin_t_ref[...] * Q_PRESCALE
    q_t = _dot_nt(wq_t_ref[...], cq)
    kn = jnp.dot(ckv, wk_ref[...], preferred_element_type=F32)
    for h in range(heads):
        lo = h * QK_PAD
        hi = q_t[lo + HEAD_DIM:lo + QK_PAD, :]
        q_t_ref[lo:lo + HEAD_DIM, :] = (q_t[lo:lo + HEAD_DIM, :] * Q_PRESCALE).astype(BF16)
        q_t_ref[lo + HEAD_DIM:lo + QK_PAD, :] = (
            hi * cos_t + pltpu.roll(hi, ROPE_DIM, axis=0) * sin_t).astype(BF16)
        k_ref[:, lo:lo + HEAD_DIM] = kn[:, h * HEAD_DIM:(h + 1) * HEAD_DIM].astype(BF16)
        k_ref[:, lo + HEAD_DIM:lo + QK_PAD] = krp
    v_t = _dot_nt(wv_t_ref[...], ckv)
    for h in range(heads):
        lo = h * V_ROWS
        v_t_ref[lo:lo + HEAD_DIM, :] = v_t[h * HEAD_DIM:(h + 1) * HEAD_DIM, :].astype(BF16)
        v_t_ref[lo + HEAD_DIM:lo + V_ROWS, :] = jnp.ones((V_ROWS - HEAD_DIM, v_t.shape[1]), BF16)


def _up_project(p_lora, tables, gq, gkv, wq_t, wk, wv_t, layer, heads, tm):
    m, pw = p_lora.shape
    ql, kvl = gq.shape[-1], gkv.shape[-1]
    depth = wq_t.shape[0]
    cos_r, sin_r, cos_t, sin_t = tables
    const = lambda i: (layer, 0, 0)
    return pl.pallas_call(
        functools.partial(_up_kernel, ql=ql, kvl=kvl, heads=heads),
        grid=(m // tm,),
        in_specs=[pl.BlockSpec((tm, pw), lambda i: (i, 0)),
                  pl.BlockSpec((tm, LANE), lambda i: (i, 0)),
                  pl.BlockSpec((tm, LANE), lambda i: (i, 0)),
                  pl.BlockSpec((LANE, tm), lambda i: (0, i)),
                  pl.BlockSpec((LANE, tm), lambda i: (0, i)),
                  pl.BlockSpec((None, 1, ql), const),
                  pl.BlockSpec((None, 1, kvl), const),
                  pl.BlockSpec((None, heads * QK_PAD, ql), const),
                  pl.BlockSpec((None, kvl, heads * HEAD_DIM), const),
                  pl.BlockSpec((None, heads * HEAD_DIM, kvl), const)],
        out_specs=[pl.BlockSpec((heads * QK_PAD, tm), lambda i: (0, i)),
                   pl.BlockSpec((tm, heads * QK_PAD), lambda i: (i, 0)),
                   pl.BlockSpec((None, heads * V_ROWS, tm), lambda i: (i, 0, 0))],
        out_shape=[jax.ShapeDtypeStruct((heads * QK_PAD, m), BF16),
                   jax.ShapeDtypeStruct((m, heads * QK_PAD), BF16),
                   jax.ShapeDtypeStruct((m // tm, heads * V_ROWS, tm), BF16)],
        compiler_params=_params(1),
        name="lora_up",
    )(p_lora, cos_r, sin_r, cos_t, sin_t, gq.reshape(depth, 1, ql), gkv.reshape(depth, 1, kvl), wq_t, wk, wv_t)


def _attn_kernel(q_t_ref, kl_ref, kc_ref, vl_ref, vc_ref, o_ref,
                 s0_ref, s1_ref, s2_ref, p0_ref, p1_ref, p2_ref, a0_ref, a1_ref, a2_ref,
                 c0_ref, c1_ref, c2_ref, m_ref, acc_ref, *, tk, unroll):
    n_lat = kl_ref.shape[0] // tk
    n = n_lat + 1
    s_buf, p_buf, a_buf = (s0_ref, s1_ref, s2_ref), (p0_ref, p1_ref, p2_ref), (a0_ref, a1_ref, a2_ref)
    c_buf = (c0_ref, c1_ref, c2_ref)
    q_t = q_t_ref[...]

    def k_chunk(c):
        if isinstance(c, int) and c == n_lat:
            return kc_ref[...]
        return kl_ref[pl.ds(pl.multiple_of(c * tk, tk), tk), :]

    def v_chunk(c):
        if isinstance(c, int) and c == n_lat:
            return vc_ref[0]
        return vl_ref[c]

    def scores(c, slot):
        s = jnp.dot(k_chunk(c), q_t, preferred_element_type=F32)
        s_buf[slot][...] = s
        c_buf[slot][...] = jnp.max(s, axis=0, keepdims=True)

    def softmax(slot):
        m_old = m_ref[...]
        m_new = jnp.maximum(m_old, c_buf[slot][...])
        alpha = jnp.exp2(m_old - m_new)
        m_ref[...] = m_new
        a_buf[slot][...] = alpha
        p_buf[slot][...] = jnp.exp2((s_buf[slot][...] - m_new).astype(BF16))

    def values(c, slot):
        acc_ref[...] = a_buf[slot][...] * acc_ref[...] + jnp.dot(
            v_chunk(c), p_buf[slot][...], preferred_element_type=F32)

    def stage(c, slot, with_scores=True, with_values=True):
        if with_scores:
            scores(c + 2, (slot + 2) % 3)
        if with_values:
            values(c - 2, (slot + 1) % 3)
        softmax(slot)

    m_ref[...] = jnp.full(m_ref.shape, -jnp.inf, F32)
    acc_ref[...] = jnp.zeros(acc_ref.shape, F32)
    scores(0, 0)
    scores(1, 1)
    for c in range(2):
        stage(c, c, with_scores=c + 2 < n, with_values=False)
    n_trips = max(n - 5, 0) // unroll

    def trip(j, carry):
        for r in range(unroll):
            stage(2 + unroll * j + r, (2 + r) % 3)
        return carry

    lax.fori_loop(0, n_trips, trip, 0)
    for c in range(2 + unroll * n_trips, n):
        stage(c, c % 3, with_scores=c + 2 < n)
    values(n - 2, (n - 2) % 3)
    values(n - 1, (n - 1) % 3)
    o_t = acc_ref[:HEAD_DIM, :] / acc_ref[HEAD_DIM:HEAD_DIM + 1, :]
    o_ref[...] = o_t.T.astype(o_ref.dtype)


def _attend_latent(q_t, k, v_t, out_rows, batch, seq, ctx_len, heads, tq, tk):
    n_q = seq // tq
    ctx0 = batch * seq // ctx_len
    n_lat = seq // tk
    return pl.pallas_call(
        functools.partial(_attn_kernel, tk=tk, unroll=ATTN_UNROLL),
        grid=(batch, heads, n_q),
        in_specs=[pl.BlockSpec((QK_PAD, tq), lambda b, h, i: (h, b * n_q + i)),
                  pl.BlockSpec((seq, QK_PAD), lambda b, h, i: (b, h)),
                  pl.BlockSpec((ctx_len, QK_PAD), lambda b, h, i: (ctx0 + b, h)),
                  pl.BlockSpec((n_lat, V_ROWS, tk), lambda b, h, i: (b, h, 0)),
                  pl.BlockSpec((ctx_len // tk, V_ROWS, tk), lambda b, h, i: (ctx0 + b, h, 0))],
        out_specs=pl.BlockSpec((tq, HEAD_DIM), lambda b, h, i: (b * n_q + i, h)),
        out_shape=jax.ShapeDtypeStruct((out_rows, heads * HEAD_DIM), BF16),
        scratch_shapes=[pltpu.VMEM((tk, tq), F32)] * 3 + [pltpu.VMEM((tk, tq), BF16)] * 3 + [
                        pltpu.VMEM((1, tq), F32)] * 6 + [
                        pltpu.VMEM((1, tq), F32), pltpu.VMEM((V_ROWS, tq), F32)],
        compiler_params=_params(3),
        name="attn_latent",
    )(q_t, k, k, v_t, v_t)


def _attn_ctx_kernel(q_t_ref, k_ref, v_t_ref, y_ref, o_ref):
    del y_ref
    s = jnp.dot(k_ref[...], q_t_ref[...], preferred_element_type=F32)
    p = jnp.exp2((s - jnp.max(s, axis=0, keepdims=True)).astype(BF16))
    o_t = jnp.dot(v_t_ref[0], p, preferred_element_type=F32)
    o_ref[...] = (o_t[:HEAD_DIM, :] / o_t[HEAD_DIM:HEAD_DIM + 1, :]).T.astype(o_ref.dtype)


def _attend_ctx(q_t, k, v_t, y_attn, batch, seq, ctx_len, heads):
    ctx0 = batch * seq // ctx_len
    return pl.pallas_call(
        _attn_ctx_kernel,
        grid=(batch, heads),
        in_specs=[pl.BlockSpec((QK_PAD, ctx_len), lambda b, h: (h, ctx0 + b)),
                  pl.BlockSpec((ctx_len, QK_PAD), lambda b, h: (ctx0 + b, h)),
                  pl.BlockSpec((1, V_ROWS, ctx_len), lambda b, h: (ctx0 + b, h, 0)),
                  pl.BlockSpec(memory_space=pl.ANY)],
        out_specs=pl.BlockSpec((ctx_len, HEAD_DIM), lambda b, h: (ctx0 + b, h)),
        out_shape=jax.ShapeDtypeStruct(y_attn.shape, y_attn.dtype),
        input_output_aliases={3: 0},
        compiler_params=_params(2),
        name="attn_ctx",
    )(q_t, k, v_t, y_attn)


def _merge_kernel(u_ref, bg_ref, cg_ref, zc_ref, zm_ref, up_ref, cp_ref, un_ref, cn_ref,
                  ya_ref, cw_ref, yc_ref, ym_ref, *, tm, lat_rows, seq, ctx_len):
    i = pl.program_id(0)
    row0 = i * tm
    row1 = row0 + tm
    is_start = jnp.where(row0 < lat_rows, row0 % seq == 0, (row0 - lat_rows) % ctx_len == 0)
    is_end = jnp.where(row1 <= lat_rows, row1 % seq == 0, (row1 - lat_rows) % ctx_len == 0)
    v = (cg_ref[...] * u_ref[...]).astype(F32)
    v_before = (cp_ref[SUBLANE - 1:SUBLANE, :] * up_ref[SUBLANE - 1:SUBLANE, :]).astype(F32)
    v_after = (cn_ref[0:1, :] * un_ref[0:1, :]).astype(F32)
    v_before = jnp.where(is_start, 0.0, v_before)
    v_after = jnp.where(is_end, 0.0, v_after)
    row = lax.broadcasted_iota(jnp.int32, v.shape, 0)
    v_prev = jnp.where(row == 0, v_before, pltpu.roll(v, 1, axis=0))
    v_next = jnp.where(row == tm - 1, v_after, pltpu.roll(v, tm - 1, axis=0))
    cw = cw_ref[...]
    y = (v_prev * cw[0:1, :] + v * cw[1:2, :] + v_next * cw[2:3, :]).astype(BF16)
    yc_ref[...] = bg_ref[...] * y * _silu(zc_ref[...])
    ym_ref[...] = ya_ref[...] * _silu(zm_ref[...])


def _merge(p_conv, p_zm, y_attn, conv_w, layer, rows, lat_rows, seq, ctx_len, tm, tc):
    m_all = p_conv.shape[0]
    cw = conv_w.shape[-1]
    nc = cw // tc
    hb = tm // SUBLANE
    last_hb = m_all // SUBLANE - 1
    main = lambda s: pl.BlockSpec((tm, tc), lambda i, j: (i, s * nc + j))
    prev = lambda s: pl.BlockSpec((SUBLANE, tc), lambda i, j: (jnp.maximum(i * hb - 1, 0), s * nc + j))
    nxt = lambda s: pl.BlockSpec((SUBLANE, tc), lambda i, j: (jnp.minimum((i + 1) * hb, last_hb), s * nc + j))
    out = pl.BlockSpec((tm, tc), lambda i, j: (i, j))
    return pl.pallas_call(
        functools.partial(_merge_kernel, tm=tm, lat_rows=lat_rows, seq=seq, ctx_len=ctx_len),
        grid=(rows // tm, nc),
        in_specs=[main(0), main(1), main(2), main(3), out,
                  prev(0), prev(2), nxt(0), nxt(2),
                  out,
                  pl.BlockSpec((None, 3, tc), lambda i, j: (layer, 0, j))],
        out_specs=[out, out],
        out_shape=[jax.ShapeDtypeStruct((rows, cw), BF16), jax.ShapeDtypeStruct((rows, cw), BF16)],
        compiler_params=_params(2),
        name="conv_merge",
    )(p_conv, p_conv, p_conv, p_conv, p_zm, p_conv, p_conv, p_conv, p_conv, y_attn, conv_w)


def _layernorm(z, g, b):
    mu = jnp.mean(z, axis=-1, keepdims=True)
    zc = z - mu
    var = jnp.mean(zc * zc, axis=-1, keepdims=True)
    return zc * lax.rsqrt(var + EPS) * g + b


def _postln_next_kernel(x_ref, o_ref, gate_ref, g_ref, b_ref, sh_ref, sc_ref, xn_ref, h_ref, *, alpha):
    xn = _layernorm(alpha * x_ref[...] + gate_ref[...] * o_ref[...], g_ref[...], b_ref[...])
    xn_ref[...] = xn
    h_ref[...] = (xn * (1.0 + sc_ref[...]) + sh_ref[...]).astype(h_ref.dtype)


def _postln_last_kernel(x_ref, o_ref, gate_ref, g_ref, b_ref, xn_ref, *, alpha):
    xn_ref[...] = _layernorm(alpha * x_ref[...] + gate_ref[...] * o_ref[...], g_ref[...], b_ref[...])


def _postln(x_all, out, mods, ln_g, ln_b, layer, rows, group_of, tm, alpha, last):
    d = x_all.shape[1]
    depth = ln_g.shape[0]
    gof = lambda i: group_of(i, tm)
    tile = pl.BlockSpec((tm, d), lambda i: (i, 0))
    vec = pl.BlockSpec((None, 1, d), lambda i: (layer, 0, 0))
    in_specs = [tile, tile, _row_spec(mods, layer, 2, gof), vec, vec]
    args = [x_all, out, mods, ln_g.reshape(depth, 1, d), ln_b.reshape(depth, 1, d)]
    if last:
        return pl.pallas_call(
            functools.partial(_postln_last_kernel, alpha=alpha),
            grid=(rows // tm,), in_specs=in_specs, out_specs=tile,
            out_shape=jax.ShapeDtypeStruct((rows, d), F32),
            compiler_params=_params(1), name="postln_last",
        )(*args)
    in_specs += [_row_spec(mods, layer + 1, 0, gof), _row_spec(mods, layer + 1, 1, gof)]
    args += [mods, mods]
    return pl.pallas_call(
        functools.partial(_postln_next_kernel, alpha=alpha),
        grid=(rows // tm,), in_specs=in_specs, out_specs=[tile, tile],
        out_shape=[jax.ShapeDtypeStruct((rows, d), F32), jax.ShapeDtypeStruct((rows, d), BF16)],
        compiler_params=_params(1), name="postln_next",
    )(*args)


def _rot_cols(w):
    shp = w.shape
    wr = w.reshape(shp[:-1] + (2, 2, ROPE_FREQ))
    return jnp.stack([-wr[..., 1, :], wr[..., 0, :]], axis=-2).reshape(shp)


def _rope_tables(batch, seq, ctx_rows):
    rows = seq // GRID_W
    row = jnp.repeat(jnp.arange(rows, dtype=F32), GRID_W)
    col = jnp.tile(jnp.arange(GRID_W, dtype=F32), rows)
    pos = jnp.stack([row, col], axis=-1)
    inv_freq = ROPE_BASE ** (-(jnp.arange(ROPE_FREQ, dtype=F32) * 2.0) / (ROPE_DIM // 2))
    ang = pos[:, :, None] * inv_freq
    expand = lambda t: jnp.broadcast_to(t[:, :, None, :], (seq, 2, 2, ROPE_FREQ)).reshape(seq, ROPE_DIM)
    cos = jnp.concatenate([jnp.tile(expand(jnp.cos(ang)), (batch, 1)), jnp.ones((ctx_rows, ROPE_DIM), F32)])
    sin = jnp.concatenate([jnp.tile(expand(jnp.sin(ang)), (batch, 1)), jnp.zeros((ctx_rows, ROPE_DIM), F32)])
    pad = lambda t: jnp.pad(t, ((0, 0), (0, LANE - ROPE_DIM)))
    cos, sin = pad(cos), pad(sin)
    return cos, sin, cos.T, sin.T


def kernel(x, c, ctx, c_ctx, w_ada, b_ada, w_in, conv_w, q_norm_g, w_uq, kv_norm_g, w_ukv, w_out, ln_g, ln_b):
    batch, seq, d = x.shape
    ctx_len = ctx.shape[1]
    depth = w_ada.shape[0]
    cw = conv_w.shape[-1]
    heads = (w_out.shape[1] - cw) // HEAD_DIM
    ql, kvl = q_norm_g.shape[-1], kv_norm_g.shape[-1]
    lat_rows, ctx_rows = batch * seq, batch * ctx_len
    m_all = lat_rows + ctx_rows
    alpha = (2.0 * depth) ** 0.25

    sec = [cw, cw, cw, cw, ql, kvl, ROPE_DIM, cw]
    offs = [0]
    for s in sec:
        offs.append(offs[-1] + s)
    w_in_bf = w_in.astype(BF16)
    col = lambda k: w_in_bf[:, :, offs[k]:offs[k + 1]]
    w_zm = col(7)
    w_lora = jnp.concatenate([w_in_bf[:, :, offs[4]:offs[7]], _rot_cols(col(6))], axis=-1)
    wq = w_uq.reshape(depth, ql, heads, HEAD_DIM + ROPE_DIM)
    wq = jnp.concatenate([wq, _rot_cols(wq[..., HEAD_DIM:])], axis=-1).reshape(depth, ql, heads * QK_PAD)
    wq_t = jnp.swapaxes(wq, 1, 2).astype(BF16)
    wkv = w_ukv.reshape(depth, kvl, heads, 2 * HEAD_DIM)
    wk = wkv[..., :HEAD_DIM].reshape(depth, kvl, heads * HEAD_DIM).astype(BF16)
    wv_t = jnp.swapaxes(wkv[..., HEAD_DIM:].reshape(depth, kvl, heads * HEAD_DIM), 1, 2).astype(BF16)
    wo_conv = w_out[:, :cw, :].astype(BF16)
    wo_mla = w_out[:, cw:, :].astype(BF16)
    tables = _rope_tables(batch, seq, ctx_rows)

    cc = jnp.zeros((SUBLANE, d), F32).at[:batch].set(c).at[batch].set(c_ctx)
    mods = _modulation(cc, w_ada, b_ada)[:, :batch + 1].reshape(depth, batch + 1, 3, 1, d)
    group_of = lambda i, tm: jnp.minimum((i * tm) // seq, batch)

    tm = _tile(math.gcd(seq, ctx_rows), (512, 256, 128))
    tm_small = _tile(math.gcd(seq, ctx_len), (256, 128))
    assert tm_small == ctx_len, "the context keys must form exactly one key chunk"
    x_all = jnp.concatenate([x.reshape(lat_rows, d), ctx.reshape(ctx_rows, d)], axis=0)
    h = _modulate(x_all, mods, 0, group_of, tm)

    for layer in range(depth):
        last = layer == depth - 1
        rows = lat_rows if last else m_all
        tn = _tile(cw, (1024, 512, 256, 128))
        p_conv = _matmul(h, w_in_bf, layer, BF16, rows, tm, _tile(4 * cw, (2048, 1024, 512, 256, 128)),
                         "in_proj_conv", n_cols=4 * cw)
        p_zm = _matmul(h, w_zm, layer, BF16, rows, tm, tn, "in_proj_zm")
        p_lora = _matmul(h, w_lora, layer, F32, m_all, tm, w_lora.shape[-1], "in_proj_lora")
        q_t, k, v_t = _up_project(p_lora, tables, q_norm_g, kv_norm_g, wq_t, wk, wv_t, layer, heads, tm_small)
        y_attn = _attend_latent(q_t, k, v_t, rows, batch, seq, ctx_len, heads,
                                _tile(seq, (2048, 1024, 512, 256, 128)), tm_small)
        if not last:
            y_attn = _attend_ctx(q_t, k, v_t, y_attn, batch, seq, ctx_len, heads)
        y_conv, y_mla = _merge(p_conv, p_zm, y_attn, conv_w, layer, rows, lat_rows, seq, ctx_len,
                               tm_small, _tile(cw, (1024, 512, 256, 128)))
        out = _matmul2(y_conv, y_mla, wo_conv, wo_mla, layer, rows, tm, _tile(d, (1024, 512, 256, 128)), "out_proj")
        if last:
            x_all = _postln(x_all, out, mods, ln_g, ln_b, layer, rows, group_of, tm_small, alpha, True)
        else:
            x_all, h = _postln(x_all, out, mods, ln_g, ln_b, layer, rows, group_of, tm_small, alpha, False)
    return x_all.reshape(batch, seq, d)
```

```python
import functools
import math

import jax
import jax.numpy as jnp
from jax import lax
from jax.experimental import pallas as pl
from jax.experimental.pallas import tpu as pltpu

F32 = jnp.float32
BF16 = jnp.bfloat16

HEAD_DIM = 128
ROPE_DIM = 64
ROPE_FREQ = ROPE_DIM // 4
ROPE_BASE = 10000.0
GRID_W = 64
QK_PAD = 2 * HEAD_DIM
V_ROWS = HEAD_DIM + 16
EPS = 1e-6
ATTN_SCALE = 1.0 / math.sqrt(HEAD_DIM + ROPE_DIM)
Q_PRESCALE = ATTN_SCALE * math.log2(math.e)
ATTN_UNROLL = 9
LANE = 128
SUBLANE = 8
VMEM_LIMIT = 56 * 1024 * 1024


def _params(n_axes, vmem=VMEM_LIMIT):
    return pltpu.CompilerParams(dimension_semantics=("arbitrary",) * n_axes,
                                vmem_limit_bytes=vmem)


def _tile(n, prefs):
    for t in prefs:
        if n % t == 0:
            return t
    return n


def _silu(x):
    h = 0.5 * x
    return h + h * jnp.tanh(h)


def _mod_kernel(a_ref, w_ref, b_ref, o_ref):
    a = _silu(a_ref[...]).astype(BF16)
    o_ref[...] = jnp.dot(a, w_ref[...].astype(BF16), preferred_element_type=F32) + b_ref[...]


def _modulation(cc, w_ada, b_ada):
    depth, d, n = w_ada.shape
    tn = _tile(n, (512, 256, 128))
    return pl.pallas_call(
        _mod_kernel,
        grid=(depth, n // tn),
        in_specs=[pl.BlockSpec((SUBLANE, d), lambda l, j: (0, 0)),
                  pl.BlockSpec((None, d, tn), lambda l, j: (l, 0, j)),
                  pl.BlockSpec((None, 1, tn), lambda l, j: (l, 0, j))],
        out_specs=pl.BlockSpec((None, SUBLANE, tn), lambda l, j: (l, 0, j)),
        out_shape=jax.ShapeDtypeStruct((depth, SUBLANE, n), F32),
        compiler_params=_params(2),
        name="adaln_mod",
    )(cc, w_ada, b_ada.reshape(depth, 1, n))


def _row_spec(mods, layer, part, group_of_tile):
    d = mods.shape[-1]
    return pl.BlockSpec((None, None, None, 1, d),
                        lambda i: (layer, group_of_tile(i), part, 0, 0))


def _modulate_kernel(x_ref, sh_ref, sc_ref, h_ref):
    h_ref[...] = (x_ref[...] * (1.0 + sc_ref[...]) + sh_ref[...]).astype(h_ref.dtype)


def _modulate(x_all, mods, layer, group_of, tm):
    m, d = x_all.shape
    gof = lambda i: group_of(i, tm)
    return pl.pallas_call(
        _modulate_kernel,
        grid=(m // tm,),
        in_specs=[pl.BlockSpec((tm, d), lambda i: (i, 0)),
                  _row_spec(mods, layer, 0, gof),
                  _row_spec(mods, layer, 1, gof)],
        out_specs=pl.BlockSpec((tm, d), lambda i: (i, 0)),
        out_shape=jax.ShapeDtypeStruct((m, d), BF16),
        compiler_params=_params(1),
        name="modulate",
    )(x_all, mods, mods)


def _mm_kernel(a_ref, b_ref, o_ref):
    o_ref[...] = jnp.dot(a_ref[...], b_ref[...], preferred_element_type=F32).astype(o_ref.dtype)


def _matmul(a, w, layer, out_dtype, rows, tm, tn, name, n_cols=None):
    k = a.shape[1]
    n = n_cols or w.shape[2]
    return pl.pallas_call(
        _mm_kernel,
        grid=(n // tn, rows // tm),
        in_specs=[pl.BlockSpec((tm, k), lambda j, i: (i, 0)),
                  pl.BlockSpec((None, k, tn), lambda j, i: (layer, 0, j))],
        out_specs=pl.BlockSpec((tm, tn), lambda j, i: (i, j)),
        out_shape=jax.ShapeDtypeStruct((rows, n), out_dtype),
        compiler_params=_params(2),
        name=name,
    )(a, w)


def _mm2_kernel(a1_ref, a2_ref, b1_ref, b2_ref, o_ref):
    acc = jnp.dot(a1_ref[...], b1_ref[...], preferred_element_type=F32)
    acc += jnp.dot(a2_ref[...], b2_ref[...], preferred_element_type=F32)
    o_ref[...] = acc.astype(o_ref.dtype)


def _matmul2(a1, a2, w1, w2, layer, rows, tm, tn, name):
    k = a1.shape[1]
    n = w1.shape[2]
    return pl.pallas_call(
        _mm2_kernel,
        grid=(n // tn, rows // tm),
        in_specs=[pl.BlockSpec((tm, k), lambda j, i: (i, 0)),
                  pl.BlockSpec((tm, k), lambda j, i: (i, 0)),
                  pl.BlockSpec((None, k, tn), lambda j, i: (layer, 0, j)),
                  pl.BlockSpec((None, k, tn), lambda j, i: (layer, 0, j))],
        out_specs=pl.BlockSpec((tm, tn), lambda j, i: (i, j)),
        out_shape=jax.ShapeDtypeStruct((rows, n), F32),
        compiler_params=_params(2),
        name=name,
    )(a1, a2, w1, w2)


def _rope_half(x, cos, sin):
    return x * cos + pltpu.roll(x, ROPE_DIM, axis=1) * sin


def _rms(x, g):
    return (x * lax.rsqrt(jnp.mean(x * x, axis=-1, keepdims=True) + EPS)) * g


def _dot_nt(a, b):
    return lax.dot_general(a, b, (((1,), (1,)), ((), ())), preferred_element_type=F32)


def _up_kernel(p_ref, cos_ref, sin_ref, cos_t_ref, sin_t_ref, gq_ref, gkv_ref, wq_t_ref, wk_ref, wv_t_ref,
               q_t_ref, k_ref, v_t_ref, *, ql, kvl, heads):
    cq = _rms(p_ref[:, :ql], gq_ref[...]).astype(BF16)
    ckv = _rms(p_ref[:, ql:ql + kvl], gkv_ref[...]).astype(BF16)
    krp = _rope_half(p_ref[:, ql + kvl:], cos_ref[...], sin_ref[...]).astype(BF16)
    cos_t = cos_t_ref[...] * Q_PRESCALE
    sin_t = sin_t_ref[...] * Q_PRESCALE
    q_t = _dot_nt(wq_t_ref[...], cq)
    kn = jnp.dot(ckv, wk_ref[...], preferred_element_type=F32)
    for h in range(heads):
        lo = h * QK_PAD
        hi = q_t[lo + HEAD_DIM:lo + QK_PAD, :]
        q_t_ref[lo:lo + HEAD_DIM, :] = (q_t[lo:lo + HEAD_DIM, :] * Q_PRESCALE).astype(BF16)
        q_t_ref[lo + HEAD_DIM:lo + QK_PAD, :] = (
            hi * cos_t + pltpu.roll(hi, ROPE_DIM, axis=0) * sin_t).astype(BF16)
        k_ref[:, lo:lo + HEAD_DIM] = kn[:, h * HEAD_DIM:(h + 1) * HEAD_DIM].astype(BF16)
        k_ref[:, lo + HEAD_DIM:lo + QK_PAD] = krp
    v_t = _dot_nt(wv_t_ref[...], ckv)
    for h in range(heads):
        lo = h * V_ROWS
        v_t_ref[lo:lo + HEAD_DIM, :] = v_t[h * HEAD_DIM:(h + 1) * HEAD_DIM, :].astype(BF16)
        v_t_ref[lo + HEAD_DIM:lo + V_ROWS, :] = jnp.ones((V_ROWS - HEAD_DIM, v_t.shape[1]), BF16)


def _up_project(p_lora, tables, gq, gkv, wq_t, wk, wv_t, layer, heads, tm):
    m, pw = p_lora.shape
    ql, kvl = gq.shape[-1], gkv.shape[-1]
    depth = wq_t.shape[0]
    cos_r, sin_r, cos_t, sin_t = tables
    const = lambda i: (layer, 0, 0)
    return pl.pallas_call(
        functools.partial(_up_kernel, ql=ql, kvl=kvl, heads=heads),
        grid=(m // tm,),
        in_specs=[pl.BlockSpec((tm, pw), lambda i: (i, 0)),
                  pl.BlockSpec((tm, LANE), lambda i: (i, 0)),
                  pl.BlockSpec((tm, LANE), lambda i: (i, 0)),
                  pl.BlockSpec((LANE, tm), lambda i: (0, i)),
                  pl.BlockSpec((LANE, tm), lambda i: (0, i)),
                  pl.BlockSpec((None, 1, ql), const),
                  pl.BlockSpec((None, 1, kvl), const),
                  pl.BlockSpec((None, heads * QK_PAD, ql), const),
                  pl.BlockSpec((None, kvl, heads * HEAD_DIM), const),
                  pl.BlockSpec((None, heads * HEAD_DIM, kvl), const)],
        out_specs=[pl.BlockSpec((heads * QK_PAD, tm), lambda i: (0, i)),
                   pl.BlockSpec((tm, heads * QK_PAD), lambda i: (i, 0)),
                   pl.BlockSpec((None, heads * V_ROWS, tm), lambda i: (i, 0, 0))],
        out_shape=[jax.ShapeDtypeStruct((heads * QK_PAD, m), BF16),
                   jax.ShapeDtypeStruct((m, heads * QK_PAD), BF16),
                   jax.ShapeDtypeStruct((m // tm, heads * V_ROWS, tm), BF16)],
        compiler_params=_params(1),
        name="lora_up",
    )(p_lora, cos_r, sin_r, cos_t, sin_t, gq.reshape(depth, 1, ql), gkv.reshape(depth, 1, kvl), wq_t, wk, wv_t)


def _attn_kernel(q_t_ref, kl_ref, kc_ref, vl_ref, vc_ref, o_ref,
                 s0_ref, s1_ref, s2_ref, p0_ref, p1_ref, p2_ref, a0_ref, a1_ref, a2_ref,
                 c0_ref, c1_ref, c2_ref, m_ref, acc_ref, *, tk, unroll):
    n_lat = kl_ref.shape[0] // tk
    n = n_lat + 1
    s_buf, p_buf, a_buf = (s0_ref, s1_ref, s2_ref), (p0_ref, p1_ref, p2_ref), (a0_ref, a1_ref, a2_ref)
    c_buf = (c0_ref, c1_ref, c2_ref)
    q_t = q_t_ref[...]

    def k_chunk(c):
        if isinstance(c, int) and c == n_lat:
            return kc_ref[...]
        return kl_ref[pl.ds(pl.multiple_of(c * tk, tk), tk), :]

    def v_chunk(c):
        if isinstance(c, int) and c == n_lat:
            return vc_ref[0]
        return vl_ref[c]

    def scores(c, slot):
        s = jnp.dot(k_chunk(c), q_t, preferred_element_type=F32)
        s_buf[slot][...] = s
        c_buf[slot][...] = jnp.max(s, axis=0, keepdims=True)

    def softmax(slot):
        m_old = m_ref[...]
        m_new = jnp.maximum(m_old, c_buf[slot][...])
        alpha = jnp.exp2(m_old - m_new)
        m_ref[...] = m_new
        a_buf[slot][...] = alpha
        p_buf[slot][...] = jnp.exp2((s_buf[slot][...] - m_new).astype(BF16))

    def values(c, slot):
        acc_ref[...] = a_buf[slot][...] * acc_ref[...] + jnp.dot(
            v_chunk(c), p_buf[slot][...], preferred_element_type=F32)

    def stage(c, slot, with_scores=True, with_values=True):
        if with_scores:
            scores(c + 2, (slot + 2) % 3)
        if with_values:
            values(c - 2, (slot + 1) % 3)
        softmax(slot)

    m_ref[...] = jnp.full(m_ref.shape, -jnp.inf, F32)
    acc_ref[...] = jnp.zeros(acc_ref.shape, F32)
    scores(0, 0)
    scores(1, 1)
    for c in range(2):
        stage(c, c, with_scores=c + 2 < n, with_values=False)
    n_trips = max(n - 5, 0) // unroll

    def trip(j, carry):
        for r in range(unroll):
            stage(2 + unroll * j + r, (2 + r) % 3)
        return carry

    lax.fori_loop(0, n_trips, trip, 0)
    for c in range(2 + unroll * n_trips, n):
        stage(c, c % 3, with_scores=c + 2 < n)
    values(n - 2, (n - 2) % 3)
    values(n - 1, (n - 1) % 3)
    o_t = acc_ref[:HEAD_DIM, :] / acc_ref[HEAD_DIM:HEAD_DIM + 1, :]
    o_ref[...] = o_t.T.astype(o_ref.dtype)


def _attend_latent(q_t, k, v_t, batch, seq, ctx_len, heads, tq, tk):
    n_q = seq // tq
    ctx0 = batch * seq // ctx_len
    n_lat = seq // tk
    return pl.pallas_call(
        functools.partial(_attn_kernel, tk=tk, unroll=ATTN_UNROLL),
        grid=(batch, heads, n_q),
        in_specs=[pl.BlockSpec((QK_PAD, tq), lambda b, h, i: (h, b * n_q + i)),
                  pl.BlockSpec((seq, QK_PAD), lambda b, h, i: (b, h)),
                  pl.BlockSpec((ctx_len, QK_PAD), lambda b, h, i: (ctx0 + b, h)),
                  pl.BlockSpec((n_lat, V_ROWS, tk), lambda b, h, i: (b, h, 0)),
                  pl.BlockSpec((ctx_len // tk, V_ROWS, tk), lambda b, h, i: (ctx0 + b, h, 0))],
        out_specs=pl.BlockSpec((tq, HEAD_DIM), lambda b, h, i: (b * n_q + i, h)),
        out_shape=jax.ShapeDtypeStruct((batch * seq, heads * HEAD_DIM), BF16),
        scratch_shapes=[pltpu.VMEM((tk, tq), F32)] * 3 + [pltpu.VMEM((tk, tq), BF16)] * 3 + [
                        pltpu.VMEM((1, tq), F32)] * 6 + [
                        pltpu.VMEM((1, tq), F32), pltpu.VMEM((V_ROWS, tq), F32)],
        compiler_params=_params(3),
        name="attn_latent",
    )(q_t, k, k, v_t, v_t)


def _attn_ctx_kernel(q_t_ref, k_ref, v_t_ref, o_ref):
    s = jnp.dot(k_ref[...], q_t_ref[...], preferred_element_type=F32)
    p = jnp.exp2((s - jnp.max(s, axis=0, keepdims=True)).astype(BF16))
    o_t = jnp.dot(v_t_ref[0], p, preferred_element_type=F32)
    o_ref[...] = (o_t[:HEAD_DIM, :] / o_t[HEAD_DIM:HEAD_DIM + 1, :]).T.astype(o_ref.dtype)


def _attend_ctx(q_t, k, v_t, batch, seq, ctx_len, heads):
    ctx0 = batch * seq // ctx_len
    return pl.pallas_call(
        _attn_ctx_kernel,
        grid=(batch, heads),
        in_specs=[pl.BlockSpec((QK_PAD, ctx_len), lambda b, h: (h, ctx0 + b)),
                  pl.BlockSpec((ctx_len, QK_PAD), lambda b, h: (ctx0 + b, h)),
                  pl.BlockSpec((1, V_ROWS, ctx_len), lambda b, h: (ctx0 + b, h, 0))],
        out_specs=pl.BlockSpec((ctx_len, HEAD_DIM), lambda b, h: (b, h)),
        out_shape=jax.ShapeDtypeStruct((batch * ctx_len, heads * HEAD_DIM), BF16),
        compiler_params=_params(2),
        name="attn_ctx",
    )(q_t, k, v_t)


def _merge_kernel(u_ref, bg_ref, cg_ref, zc_ref, zm_ref, up_ref, cp_ref, un_ref, cn_ref,
                  ya_lat_ref, ya_ctx_ref, cw_ref, yc_ref, ym_ref, *, tm, lat_rows, seq, ctx_len):
    i = pl.program_id(0)
    row0 = i * tm
    row1 = row0 + tm
    is_start = jnp.where(row0 < lat_rows, row0 % seq == 0, (row0 - lat_rows) % ctx_len == 0)
    is_end = jnp.where(row1 <= lat_rows, row1 % seq == 0, (row1 - lat_rows) % ctx_len == 0)
    v = (cg_ref[...] * u_ref[...]).astype(F32)
    v_before = (cp_ref[SUBLANE - 1:SUBLANE, :] * up_ref[SUBLANE - 1:SUBLANE, :]).astype(F32)
    v_after = (cn_ref[0:1, :] * un_ref[0:1, :]).astype(F32)
    v_before = jnp.where(is_start, 0.0, v_before)
    v_after = jnp.where(is_end, 0.0, v_after)
    row = lax.broadcasted_iota(jnp.int32, v.shape, 0)
    v_prev = jnp.where(row == 0, v_before, pltpu.roll(v, 1, axis=0))
    v_next = jnp.where(row == tm - 1, v_after, pltpu.roll(v, tm - 1, axis=0))
    cw = cw_ref[...]
    y = (v_prev * cw[0:1, :] + v * cw[1:2, :] + v_next * cw[2:3, :]).astype(BF16)
    yc_ref[...] = bg_ref[...] * y * _silu(zc_ref[...])
    ya = jnp.where(row0 < lat_rows, ya_lat_ref[...], ya_ctx_ref[...])
    ym_ref[...] = ya * _silu(zm_ref[...])


def _merge(p_conv, p_zm, y_lat, y_ctx, conv_w, layer, rows, lat_rows, seq, ctx_len, tm, tc):
    m_all = p_conv.shape[0]
    cw = conv_w.shape[-1]
    nc = cw // tc
    hb = tm // SUBLANE
    last_hb = m_all // SUBLANE - 1
    n_lat = lat_rows // tm
    ya_lat = pl.BlockSpec((tm, tc), lambda i, j: (jnp.minimum(i, n_lat - 1), j))
    ya_ctx = pl.BlockSpec((tm, tc), lambda i, j: (jnp.maximum(i - n_lat, 0), j))
    main = lambda s: pl.BlockSpec((tm, tc), lambda i, j: (i, s * nc + j))
    prev = lambda s: pl.BlockSpec((SUBLANE, tc), lambda i, j: (jnp.maximum(i * hb - 1, 0), s * nc + j))
    nxt = lambda s: pl.BlockSpec((SUBLANE, tc), lambda i, j: (jnp.minimum((i + 1) * hb, last_hb), s * nc + j))
    out = pl.BlockSpec((tm, tc), lambda i, j: (i, j))
    return pl.pallas_call(
        functools.partial(_merge_kernel, tm=tm, lat_rows=lat_rows, seq=seq, ctx_len=ctx_len),
        grid=(rows // tm, nc),
        in_specs=[main(0), main(1), main(2), main(3), out,
                  prev(0), prev(2), nxt(0), nxt(2),
                  ya_lat, ya_ctx,
                  pl.BlockSpec((None, 3, tc), lambda i, j: (layer, 0, j))],
        out_specs=[out, out],
        out_shape=[jax.ShapeDtypeStruct((rows, cw), BF16), jax.ShapeDtypeStruct((rows, cw), BF16)],
        compiler_params=_params(2),
        name="conv_merge",
    )(p_conv, p_conv, p_conv, p_conv, p_zm, p_conv, p_conv, p_conv, p_conv, y_lat, y_ctx, conv_w)


def _layernorm(z, g, b):
    mu = jnp.mean(z, axis=-1, keepdims=True)
    zc = z - mu
    var = jnp.mean(zc * zc, axis=-1, keepdims=True)
    return zc * lax.rsqrt(var + EPS) * g + b


def _postln_next_kernel(x_ref, o_ref, gate_ref, g_ref, b_ref, sh_ref, sc_ref, xn_ref, h_ref, *, alpha):
    xn = _layernorm(alpha * x_ref[...] + gate_ref[...] * o_ref[...], g_ref[...], b_ref[...])
    xn_ref[...] = xn
    h_ref[...] = (xn * (1.0 + sc_ref[...]) + sh_ref[...]).astype(h_ref.dtype)


def _postln_last_kernel(x_ref, o_ref, gate_ref, g_ref, b_ref, xn_ref, *, alpha):
    xn_ref[...] = _layernorm(alpha * x_ref[...] + gate_ref[...] * o_ref[...], g_ref[...], b_ref[...])


def _postln(x_all, out, mods, ln_g, ln_b, layer, rows, group_of, tm, alpha, last):
    d = x_all.shape[1]
    depth = ln_g.shape[0]
    gof = lambda i: group_of(i, tm)
    tile = pl.BlockSpec((tm, d), lambda i: (i, 0))
    vec = pl.BlockSpec((None, 1, d), lambda i: (layer, 0, 0))
    in_specs = [tile, tile, _row_spec(mods, layer, 2, gof), vec, vec]
    args = [x_all, out, mods, ln_g.reshape(depth, 1, d), ln_b.reshape(depth, 1, d)]
    if last:
        return pl.pallas_call(
            functools.partial(_postln_last_kernel, alpha=alpha),
            grid=(rows // tm,), in_specs=in_specs, out_specs=tile,
            out_shape=jax.ShapeDtypeStruct((rows, d), F32),
            compiler_params=_params(1), name="postln_last",
        )(*args)
    in_specs += [_row_spec(mods, layer + 1, 0, gof), _row_spec(mods, layer + 1, 1, gof)]
    args += [mods, mods]
    return pl.pallas_call(
        functools.partial(_postln_next_kernel, alpha=alpha),
        grid=(rows // tm,), in_specs=in_specs, out_specs=[tile, tile],
        out_shape=[jax.ShapeDtypeStruct((rows, d), F32), jax.ShapeDtypeStruct((rows, d), BF16)],
        compiler_params=_params(1), name="postln_next",
    )(*args)


def _rot_cols(w):
    shp = w.shape
    wr = w.reshape(shp[:-1] + (2, 2, ROPE_FREQ))
    return jnp.stack([-wr[..., 1, :], wr[..., 0, :]], axis=-2).reshape(shp)


def _rope_tables(batch, seq, ctx_rows):
    rows = seq // GRID_W
    row = jnp.repeat(jnp.arange(rows, dtype=F32), GRID_W)
    col = jnp.tile(jnp.arange(GRID_W, dtype=F32), rows)
    pos = jnp.stack([row, col], axis=-1)
    inv_freq = ROPE_BASE ** (-(jnp.arange(ROPE_FREQ, dtype=F32) * 2.0) / (ROPE_DIM // 2))
    ang = pos[:, :, None] * inv_freq
    expand = lambda t: jnp.broadcast_to(t[:, :, None, :], (seq, 2, 2, ROPE_FREQ)).reshape(seq, ROPE_DIM)
    cos = jnp.concatenate([jnp.tile(expand(jnp.cos(ang)), (batch, 1)), jnp.ones((ctx_rows, ROPE_DIM), F32)])
    sin = jnp.concatenate([jnp.tile(expand(jnp.sin(ang)), (batch, 1)), jnp.zeros((ctx_rows, ROPE_DIM), F32)])
    pad = lambda t: jnp.pad(t, ((0, 0), (0, LANE - ROPE_DIM)))
    cos, sin = pad(cos), pad(sin)
    return cos, sin, cos.T, sin.T


def kernel(x, c, ctx, c_ctx, w_ada, b_ada, w_in, conv_w, q_norm_g, w_uq, kv_norm_g, w_ukv, w_out, ln_g, ln_b):
    batch, seq, d = x.shape
    ctx_len = ctx.shape[1]
    depth = w_ada.shape[0]
    cw = conv_w.shape[-1]
    heads = (w_out.shape[1] - cw) // HEAD_DIM
    ql, kvl = q_norm_g.shape[-1], kv_norm_g.shape[-1]
    lat_rows, ctx_rows = batch * seq, batch * ctx_len
    m_all = lat_rows + ctx_rows
    alpha = (2.0 * depth) ** 0.25

    sec = [cw, cw, cw, cw, ql, kvl, ROPE_DIM, cw]
    offs = [0]
    for s in sec:
        offs.append(offs[-1] + s)
    w_in_bf = w_in.astype(BF16)
    col = lambda k: w_in_bf[:, :, offs[k]:offs[k + 1]]
    w_zm = col(7)
    w_lora = jnp.concatenate([w_in_bf[:, :, offs[4]:offs[7]], _rot_cols(col(6))], axis=-1)
    wq = w_uq.reshape(depth, ql, heads, HEAD_DIM + ROPE_DIM)
    wq = jnp.concatenate([wq, _rot_cols(wq[..., HEAD_DIM:])], axis=-1).reshape(depth, ql, heads * QK_PAD)
    wq_t = jnp.swapaxes(wq, 1, 2).astype(BF16)
    wkv = w_ukv.reshape(depth, kvl, heads, 2 * HEAD_DIM)
    wk = wkv[..., :HEAD_DIM].reshape(depth, kvl, heads * HEAD_DIM).astype(BF16)
    wv_t = jnp.swapaxes(wkv[..., HEAD_DIM:].reshape(depth, kvl, heads * HEAD_DIM), 1, 2).astype(BF16)
    wo_conv = w_out[:, :cw, :].astype(BF16)
    wo_mla = w_out[:, cw:, :].astype(BF16)
    tables = _rope_tables(batch, seq, ctx_rows)

    cc = jnp.zeros((SUBLANE, d), F32).at[:batch].set(c).at[batch].set(c_ctx)
    mods = _modulation(cc, w_ada, b_ada)[:, :batch + 1].reshape(depth, batch + 1, 3, 1, d)
    group_of = lambda i, tm: jnp.minimum((i * tm) // seq, batch)

    tm = _tile(math.gcd(seq, ctx_rows), (512, 256, 128))
    tm_small = _tile(math.gcd(seq, ctx_len), (256, 128))
    assert tm_small == ctx_len, "the context keys must form exactly one key chunk"
    x_all = jnp.concatenate([x.reshape(lat_rows, d), ctx.reshape(ctx_rows, d)], axis=0)
    h = _modulate(x_all, mods, 0, group_of, tm)

    for layer in range(depth):
        last = layer == depth - 1
        rows = lat_rows if last else m_all
        tn = _tile(cw, (1024, 512, 256, 128))
        p_conv = _matmul(h, w_in_bf, layer, BF16, rows, tm, _tile(4 * cw, (2048, 1024, 512, 256, 128)),
                         "in_proj_conv", n_cols=4 * cw)
        p_zm = _matmul(h, w_zm, layer, BF16, rows, tm, tn, "in_proj_zm")
        p_lora = _matmul(h, w_lora, layer, F32, m_all, tm, w_lora.shape[-1], "in_proj_lora")
        q_t, k, v_t = _up_project(p_lora, tables, q_norm_g, kv_norm_g, wq_t, wk, wv_t, layer, heads, tm_small)
        y_lat = _attend_latent(q_t, k, v_t, batch, seq, ctx_len, heads,
                               _tile(seq, (2048, 1024, 512, 256, 128)), tm_small)
        y_ctx = y_lat if last else _attend_ctx(q_t, k, v_t, batch, seq, ctx_len, heads)
        y_conv, y_mla = _merge(p_conv, p_zm, y_lat, y_ctx, conv_w, layer, rows, lat_rows, seq, ctx_len,
                               tm_small, _tile(cw, (1024, 512, 256, 128)))
        out = _matmul2(y_conv, y_mla, wo_conv, wo_mla, layer, rows, tm, _tile(d, (1024, 512, 256, 128)), "out_proj")
        if last:
            x_all = _postln(x_all, out, mods, ln_g, ln_b, layer, rows, group_of, tm_small, alpha, True)
        else:
            x_all, h = _postln(x_all, out, mods, ln_g, ln_b, layer, rows, group_of, tm_small, alpha, False)
    return x_all.reshape(batch, seq, d)
```

```python
import functools
import math

import jax
import jax.numpy as jnp
from jax import lax
from jax.experimental import pallas as pl
from jax.experimental.pallas import tpu as pltpu

F32 = jnp.float32
BF16 = jnp.bfloat16

HEAD_DIM = 128
ROPE_DIM = 64
ROPE_FREQ = ROPE_DIM // 4
ROPE_BASE = 10000.0
GRID_W = 64
QK_PAD = 2 * HEAD_DIM
V_ROWS = HEAD_DIM + 16
EPS = 1e-6
ATTN_SCALE = 1.0 / math.sqrt(HEAD_DIM + ROPE_DIM)
Q_PRESCALE = ATTN_SCALE * math.log2(math.e)
ATTN_UNROLL = 9
LANE = 128
SUBLANE = 8
VMEM_LIMIT = 56 * 1024 * 1024


def _params(n_axes, vmem=VMEM_LIMIT):
    return pltpu.CompilerParams(dimension_semantics=("arbitrary",) * n_axes,
                                vmem_limit_bytes=vmem)


def _tile(n, prefs):
    for t in prefs:
        if n % t == 0:
            return t
    return n


def _silu(x):
    h = 0.5 * x
    return h + h * jnp.tanh(h)


def _mod_kernel(a_ref, w_ref, b_ref, o_ref):
    a = _silu(a_ref[...]).astype(BF16)
    o_ref[...] = jnp.dot(a, w_ref[...].astype(BF16), preferred_element_type=F32) + b_ref[...]


def _modulation(cc, w_ada, b_ada):
    depth, d, n = w_ada.shape
    tn = _tile(n, (512, 256, 128))
    return pl.pallas_call(
        _mod_kernel,
        grid=(depth, n // tn),
        in_specs=[pl.BlockSpec((SUBLANE, d), lambda l, j: (0, 0)),
                  pl.BlockSpec((None, d, tn), lambda l, j: (l, 0, j)),
                  pl.BlockSpec((None, 1, tn), lambda l, j: (l, 0, j))],
        out_specs=pl.BlockSpec((None, SUBLANE, tn), lambda l, j: (l, 0, j)),
        out_shape=jax.ShapeDtypeStruct((depth, SUBLANE, n), F32),
        compiler_params=_params(2),
        name="adaln_mod",
    )(cc, w_ada, b_ada.reshape(depth, 1, n))


def _row_spec(mods, layer, part, group_of_tile):
    d = mods.shape[-1]
    return pl.BlockSpec((None, None, None, 1, d),
                        lambda i: (layer, group_of_tile(i), part, 0, 0))


def _row_source_specs(tm, d, lat_rows, ctx_first_tile):
    n_lat = lat_rows // tm
    return [pl.BlockSpec((tm, d), lambda i: (jnp.minimum(i, n_lat - 1), 0)),
            pl.BlockSpec((tm, d), lambda i: (jnp.maximum(i - n_lat, 0) + ctx_first_tile, 0))]


def _pick_rows(xl_ref, xc_ref, tm, lat_rows):
    return jnp.where(pl.program_id(0) * tm < lat_rows, xl_ref[...], xc_ref[...])


def _modulate_kernel(xl_ref, xc_ref, sh_ref, sc_ref, h_ref, *, tm, lat_rows):
    x = _pick_rows(xl_ref, xc_ref, tm, lat_rows)
    h_ref[...] = (x * (1.0 + sc_ref[...]) + sh_ref[...]).astype(h_ref.dtype)


def _modulate(x_src, mods, layer, rows, lat_rows, group_of, tm):
    x_lat, x_ctx, ctx_first_tile = x_src
    d = x_lat.shape[1]
    gof = lambda i: group_of(i, tm)
    return pl.pallas_call(
        functools.partial(_modulate_kernel, tm=tm, lat_rows=lat_rows),
        grid=(rows // tm,),
        in_specs=_row_source_specs(tm, d, lat_rows, ctx_first_tile) + [
                  _row_spec(mods, layer, 0, gof),
                  _row_spec(mods, layer, 1, gof)],
        out_specs=pl.BlockSpec((tm, d), lambda i: (i, 0)),
        out_shape=jax.ShapeDtypeStruct((rows, d), BF16),
        compiler_params=_params(1),
        name="modulate",
    )(x_lat, x_ctx, mods, mods)


def _mm_kernel(a_ref, b_ref, o_ref):
    o_ref[...] = jnp.dot(a_ref[...], b_ref[...], preferred_element_type=F32).astype(o_ref.dtype)


def _matmul(a, w, layer, out_dtype, rows, tm, tn, name, n_cols=None):
    k = a.shape[1]
    n = n_cols or w.shape[2]
    return pl.pallas_call(
        _mm_kernel,
        grid=(n // tn, rows // tm),
        in_specs=[pl.BlockSpec((tm, k), lambda j, i: (i, 0)),
                  pl.BlockSpec((None, k, tn), lambda j, i: (layer, 0, j))],
        out_specs=pl.BlockSpec((tm, tn), lambda j, i: (i, j)),
        out_shape=jax.ShapeDtypeStruct((rows, n), out_dtype),
        compiler_params=_params(2),
        name=name,
    )(a, w)


def _mm2_kernel(a1_ref, a2_ref, w1_ref, w2_ref, o_ref, wb1_ref, wb2_ref):
    @pl.when(pl.program_id(1) == 0)
    def _():
        wb1_ref[...] = w1_ref[...].astype(BF16)
        wb2_ref[...] = w2_ref[...].astype(BF16)

    acc = jnp.dot(a1_ref[...], wb1_ref[...], preferred_element_type=F32)
    acc += jnp.dot(a2_ref[...], wb2_ref[...], preferred_element_type=F32)
    o_ref[...] = acc.astype(o_ref.dtype)


def _matmul2(a1, a2, w, layer, rows, tm, tn, name):
    k = a1.shape[1]
    n = w.shape[2]
    return pl.pallas_call(
        _mm2_kernel,
        grid=(n // tn, rows // tm),
        in_specs=[pl.BlockSpec((tm, k), lambda j, i: (i, 0)),
                  pl.BlockSpec((tm, k), lambda j, i: (i, 0)),
                  pl.BlockSpec((None, k, tn), lambda j, i: (layer, 0, j)),
                  pl.BlockSpec((None, k, tn), lambda j, i: (layer, 1, j))],
        out_specs=pl.BlockSpec((tm, tn), lambda j, i: (i, j)),
        out_shape=jax.ShapeDtypeStruct((rows, n), F32),
        scratch_shapes=[pltpu.VMEM((k, tn), BF16), pltpu.VMEM((k, tn), BF16)],
        compiler_params=_params(2),
        name=name,
    )(a1, a2, w, w)


def _rope_half(x, cos, sin):
    return x * cos + pltpu.roll(x, ROPE_DIM, axis=1) * sin


def _rms(x, g):
    return (x * lax.rsqrt(jnp.mean(x * x, axis=-1, keepdims=True) + EPS)) * g


def _dot_nt(a, b):
    return lax.dot_general(a, b, (((1,), (1,)), ((), ())), preferred_element_type=F32)


def _up_kernel(p_ref, cos_ref, sin_ref, cos_t_ref, sin_t_ref, gq_ref, gkv_ref, wq_t_ref, wk_ref, wv_t_ref,
               q_t_ref, k_ref, v_t_ref, *, ql, kvl, heads):
    cq = _rms(p_ref[:, :ql], gq_ref[...]).astype(BF16)
    ckv = _rms(p_ref[:, ql:ql + kvl], gkv_ref[...]).astype(BF16)
    krp = _rope_half(p_ref[:, ql + kvl:], cos_ref[...], sin_ref[...]).astype(BF16)
    cos_t = cos_t_ref[...] * Q_PRESCALE
    sin_t = sin_t_ref[...] * Q_PRESCALE
    q_t = _dot_nt(wq_t_ref[...], cq)
    kn = jnp.dot(ckv, wk_ref[...], preferred_element_type=F32)
    for h in range(heads):
        lo = h * QK_PAD
        hi = q_t[lo + HEAD_DIM:lo + QK_PAD, :]
        q_t_ref[lo:lo + HEAD_DIM, :] = (q_t[lo:lo + HEAD_DIM, :] * Q_PRESCALE).astype(BF16)
        q_t_ref[lo + HEAD_DIM:lo + QK_PAD, :] = (
            hi * cos_t + pltpu.roll(hi, ROPE_DIM, axis=0) * sin_t).astype(BF16)
        k_ref[:, lo:lo + HEAD_DIM] = kn[:, h * HEAD_DIM:(h + 1) * HEAD_DIM].astype(BF16)
        k_ref[:, lo + HEAD_DIM:lo + QK_PAD] = krp
    v_t = _dot_nt(wv_t_ref[...], ckv)
    for h in range(heads):
        lo = h * V_ROWS
        v_t_ref[lo:lo + HEAD_DIM, :] = v_t[h * HEAD_DIM:(h + 1) * HEAD_DIM, :].astype(BF16)
        v_t_ref[lo + HEAD_DIM:lo + V_ROWS, :] = jnp.ones((V_ROWS - HEAD_DIM, v_t.shape[1]), BF16)


def _up_project(p_lora, tables, gq, gkv, wq_t, wk, wv_t, layer, heads, tm):
    m, pw = p_lora.shape
    ql, kvl = gq.shape[-1], gkv.shape[-1]
    depth = wq_t.shape[0]
    cos_r, sin_r, cos_t, sin_t = tables
    const = lambda i: (layer, 0, 0)
    return pl.pallas_call(
        functools.partial(_up_kernel, ql=ql, kvl=kvl, heads=heads),
        grid=(m // tm,),
        in_specs=[pl.BlockSpec((tm, pw), lambda i: (i, 0)),
                  pl.BlockSpec((tm, LANE), lambda i: (i, 0)),
                  pl.BlockSpec((tm, LANE), lambda i: (i, 0)),
                  pl.BlockSpec((LANE, tm), lambda i: (0, i)),
                  pl.BlockSpec((LANE, tm), lambda i: (0, i)),
                  pl.BlockSpec((None, 1, ql), const),
                  pl.BlockSpec((None, 1, kvl), const),
                  pl.BlockSpec((None, heads * QK_PAD, ql), const),
                  pl.BlockSpec((None, kvl, heads * HEAD_DIM), const),
                  pl.BlockSpec((None, heads * HEAD_DIM, kvl), const)],
        out_specs=[pl.BlockSpec((heads * QK_PAD, tm), lambda i: (0, i)),
                   pl.BlockSpec((tm, heads * QK_PAD), lambda i: (i, 0)),
                   pl.BlockSpec((None, heads * V_ROWS, tm), lambda i: (i, 0, 0))],
        out_shape=[jax.ShapeDtypeStruct((heads * QK_PAD, m), BF16),
                   jax.ShapeDtypeStruct((m, heads * QK_PAD), BF16),
                   jax.ShapeDtypeStruct((m // tm, heads * V_ROWS, tm), BF16)],
        compiler_params=_params(1),
        name="lora_up",
    )(p_lora, cos_r, sin_r, cos_t, sin_t, gq.reshape(depth, 1, ql), gkv.reshape(depth, 1, kvl), wq_t, wk, wv_t)


def _attn_kernel(q_t_ref, kl_ref, kc_ref, vl_ref, vc_ref, o_ref,
                 s0_ref, s1_ref, s2_ref, p0_ref, p1_ref, p2_ref, a0_ref, a1_ref, a2_ref,
                 c0_ref, c1_ref, c2_ref, m_ref, acc_ref, *, tk, unroll):
    n_lat = kl_ref.shape[0] // tk
    n = n_lat + 1
    s_buf, p_buf, a_buf = (s0_ref, s1_ref, s2_ref), (p0_ref, p1_ref, p2_ref), (a0_ref, a1_ref, a2_ref)
    c_buf = (c0_ref, c1_ref, c2_ref)
    q_t = q_t_ref[...]

    def k_chunk(c):
        if isinstance(c, int) and c == n_lat:
            return kc_ref[...]
        return kl_ref[pl.ds(pl.multiple_of(c * tk, tk), tk), :]

    def v_chunk(c):
        if isinstance(c, int) and c == n_lat:
            return vc_ref[0]
        return vl_ref[c]

    def scores(c, slot):
        s = jnp.dot(k_chunk(c), q_t, preferred_element_type=F32)
        s_buf[slot][...] = s
        c_buf[slot][...] = jnp.max(s, axis=0, keepdims=True)

    def softmax(slot):
        m_old = m_ref[...]
        m_new = jnp.maximum(m_old, c_buf[slot][...])
        alpha = jnp.exp2(m_old - m_new)
        m_ref[...] = m_new
        a_buf[slot][...] = alpha
        p_buf[slot][...] = jnp.exp2((s_buf[slot][...] - m_new).astype(BF16))

    def values(c, slot):
        acc_ref[...] = a_buf[slot][...] * acc_ref[...] + jnp.dot(
            v_chunk(c), p_buf[slot][...], preferred_element_type=F32)

    def stage(c, slot, with_scores=True, with_values=True):
        if with_scores:
            scores(c + 2, (slot + 2) % 3)
        if with_values:
            values(c - 2, (slot + 1) % 3)
        softmax(slot)

    m_ref[...] = jnp.full(m_ref.shape, -jnp.inf, F32)
    acc_ref[...] = jnp.zeros(acc_ref.shape, F32)
    scores(0, 0)
    scores(1, 1)
    for c in range(2):
        stage(c, c, with_scores=c + 2 < n, with_values=False)
    n_trips = max(n - 5, 0) // unroll

    def trip(j, carry):
        for r in range(unroll):
            stage(2 + unroll * j + r, (2 + r) % 3)
        return carry

    lax.fori_loop(0, n_trips, trip, 0)
    for c in range(2 + unroll * n_trips, n):
        stage(c, c % 3, with_scores=c + 2 < n)
    values(n - 2, (n - 2) % 3)
    values(n - 1, (n - 1) % 3)
    o_t = acc_ref[:HEAD_DIM, :] / acc_ref[HEAD_DIM:HEAD_DIM + 1, :]
    o_ref[...] = o_t.T.astype(o_ref.dtype)


def _attend_latent(q_t, k, v_t, batch, seq, ctx_len, heads, tq, tk):
    n_q = seq // tq
    ctx0 = batch * seq // ctx_len
    n_lat = seq // tk
    return pl.pallas_call(
        functools.partial(_attn_kernel, tk=tk, unroll=ATTN_UNROLL),
        grid=(batch, heads, n_q),
        in_specs=[pl.BlockSpec((QK_PAD, tq), lambda b, h, i: (h, b * n_q + i)),
                  pl.BlockSpec((seq, QK_PAD), lambda b, h, i: (b, h)),
                  pl.BlockSpec((ctx_len, QK_PAD), lambda b, h, i: (ctx0 + b, h)),
                  pl.BlockSpec((n_lat, V_ROWS, tk), lambda b, h, i: (b, h, 0)),
                  pl.BlockSpec((ctx_len // tk, V_ROWS, tk), lambda b, h, i: (ctx0 + b, h, 0))],
        out_specs=pl.BlockSpec((tq, HEAD_DIM), lambda b, h, i: (b * n_q + i, h)),
        out_shape=jax.ShapeDtypeStruct((batch * seq, heads * HEAD_DIM), BF16),
        scratch_shapes=[pltpu.VMEM((tk, tq), F32)] * 3 + [pltpu.VMEM((tk, tq), BF16)] * 3 + [
                        pltpu.VMEM((1, tq), F32)] * 6 + [
                        pltpu.VMEM((1, tq), F32), pltpu.VMEM((V_ROWS, tq), F32)],
        compiler_params=_params(3),
        name="attn_latent",
    )(q_t, k, k, v_t, v_t)


def _attn_ctx_kernel(q_t_ref, k_ref, v_t_ref, o_ref):
    s = jnp.dot(k_ref[...], q_t_ref[...], preferred_element_type=F32)
    p = jnp.exp2((s - jnp.max(s, axis=0, keepdims=True)).astype(BF16))
    o_t = jnp.dot(v_t_ref[0], p, preferred_element_type=F32)
    o_ref[...] = (o_t[:HEAD_DIM, :] / o_t[HEAD_DIM:HEAD_DIM + 1, :]).T.astype(o_ref.dtype)


def _attend_ctx(q_t, k, v_t, batch, seq, ctx_len, heads):
    ctx0 = batch * seq // ctx_len
    return pl.pallas_call(
        _attn_ctx_kernel,
        grid=(batch, heads),
        in_specs=[pl.BlockSpec((QK_PAD, ctx_len), lambda b, h: (h, ctx0 + b)),
                  pl.BlockSpec((ctx_len, QK_PAD), lambda b, h: (ctx0 + b, h)),
                  pl.BlockSpec((1, V_ROWS, ctx_len), lambda b, h: (ctx0 + b, h, 0))],
        out_specs=pl.BlockSpec((ctx_len, HEAD_DIM), lambda b, h: (b, h)),
        out_shape=jax.ShapeDtypeStruct((batch * ctx_len, heads * HEAD_DIM), BF16),
        compiler_params=_params(2),
        name="attn_ctx",
    )(q_t, k, v_t)


def _merge_kernel(u_ref, bg_ref, cg_ref, zc_ref, zm_ref, up_ref, cp_ref, un_ref, cn_ref,
                  ya_lat_ref, ya_ctx_ref, cw_ref, yc_ref, ym_ref, *, tm, lat_rows, seq, ctx_len):
    i = pl.program_id(0)
    row0 = i * tm
    row1 = row0 + tm
    is_start = jnp.where(row0 < lat_rows, row0 % seq == 0, (row0 - lat_rows) % ctx_len == 0)
    is_end = jnp.where(row1 <= lat_rows, row1 % seq == 0, (row1 - lat_rows) % ctx_len == 0)
    v = (cg_ref[...] * u_ref[...]).astype(F32)
    v_before = (cp_ref[SUBLANE - 1:SUBLANE, :] * up_ref[SUBLANE - 1:SUBLANE, :]).astype(F32)
    v_after = (cn_ref[0:1, :] * un_ref[0:1, :]).astype(F32)
    v_before = jnp.where(is_start, 0.0, v_before)
    v_after = jnp.where(is_end, 0.0, v_after)
    row = lax.broadcasted_iota(jnp.int32, v.shape, 0)
    v_prev = jnp.where(row == 0, v_before, pltpu.roll(v, 1, axis=0))
    v_next = jnp.where(row == tm - 1, v_after, pltpu.roll(v, tm - 1, axis=0))
    cw = cw_ref[...]
    y = (v_prev * cw[0:1, :] + v * cw[1:2, :] + v_next * cw[2:3, :]).astype(BF16)
    yc_ref[...] = bg_ref[...] * y * _silu(zc_ref[...])
    ya = jnp.where(row0 < lat_rows, ya_lat_ref[...], ya_ctx_ref[...])
    ym_ref[...] = ya * _silu(zm_ref[...])


def _merge(p_conv, p_zm, y_lat, y_ctx, conv_w, layer, rows, lat_rows, seq, ctx_len, tm, tc):
    m_all = p_conv.shape[0]
    cw = conv_w.shape[-1]
    nc = cw // tc
    hb = tm // SUBLANE
    last_hb = m_all // SUBLANE - 1
    n_lat = lat_rows // tm
    ya_lat = pl.BlockSpec((tm, tc), lambda i, j: (jnp.minimum(i, n_lat - 1), j))
    ya_ctx = pl.BlockSpec((tm, tc), lambda i, j: (jnp.maximum(i - n_lat, 0), j))
    main = lambda s: pl.BlockSpec((tm, tc), lambda i, j: (i, s * nc + j))
    prev = lambda s: pl.BlockSpec((SUBLANE, tc), lambda i, j: (jnp.maximum(i * hb - 1, 0), s * nc + j))
    nxt = lambda s: pl.BlockSpec((SUBLANE, tc), lambda i, j: (jnp.minimum((i + 1) * hb, last_hb), s * nc + j))
    out = pl.BlockSpec((tm, tc), lambda i, j: (i, j))
    return pl.pallas_call(
        functools.partial(_merge_kernel, tm=tm, lat_rows=lat_rows, seq=seq, ctx_len=ctx_len),
        grid=(rows // tm, nc),
        in_specs=[main(0), main(1), main(2), main(3), out,
                  prev(0), prev(2), nxt(0), nxt(2),
                  ya_lat, ya_ctx,
                  pl.BlockSpec((None, 3, tc), lambda i, j: (layer, 0, j))],
        out_specs=[out, out],
        out_shape=[jax.ShapeDtypeStruct((rows, cw), BF16), jax.ShapeDtypeStruct((rows, cw), BF16)],
        compiler_params=_params(2),
        name="conv_merge",
    )(p_conv, p_conv, p_conv, p_conv, p_zm, p_conv, p_conv, p_conv, p_conv, y_lat, y_ctx, conv_w)


def _layernorm(z, g, b):
    mu = jnp.mean(z, axis=-1, keepdims=True)
    zc = z - mu
    var = jnp.mean(zc * zc, axis=-1, keepdims=True)
    return zc * lax.rsqrt(var + EPS) * g + b


def _postln_next_kernel(xl_ref, xc_ref, o_ref, gate_ref, g_ref, b_ref, sh_ref, sc_ref, xn_ref, h_ref,
                        *, alpha, tm, lat_rows):
    x = _pick_rows(xl_ref, xc_ref, tm, lat_rows)
    xn = _layernorm(alpha * x + gate_ref[...] * o_ref[...], g_ref[...], b_ref[...])
    xn_ref[...] = xn
    h_ref[...] = (xn * (1.0 + sc_ref[...]) + sh_ref[...]).astype(h_ref.dtype)


def _postln_last_kernel(xl_ref, xc_ref, o_ref, gate_ref, g_ref, b_ref, xn_ref, *, alpha, tm, lat_rows):
    x = _pick_rows(xl_ref, xc_ref, tm, lat_rows)
    xn_ref[...] = _layernorm(alpha * x + gate_ref[...] * o_ref[...], g_ref[...], b_ref[...])


def _postln(x_src, out, mods, ln_g, ln_b, layer, rows, lat_rows, group_of, tm, alpha, last):
    x_lat, x_ctx, ctx_first_tile = x_src
    d = x_lat.shape[1]
    depth = ln_g.shape[0]
    gof = lambda i: group_of(i, tm)
    tile = pl.BlockSpec((tm, d), lambda i: (i, 0))
    vec = pl.BlockSpec((None, 1, d), lambda i: (layer, 0, 0))
    in_specs = _row_source_specs(tm, d, lat_rows, ctx_first_tile) + [
        tile, _row_spec(mods, layer, 2, gof), vec, vec]
    args = [x_lat, x_ctx, out, mods, ln_g.reshape(depth, 1, d), ln_b.reshape(depth, 1, d)]
    if last:
        return pl.pallas_call(
            functools.partial(_postln_last_kernel, alpha=alpha, tm=tm, lat_rows=lat_rows),
            grid=(rows // tm,), in_specs=in_specs, out_specs=tile,
            out_shape=jax.ShapeDtypeStruct((rows, d), F32),
            compiler_params=_params(1), name="postln_last",
        )(*args)
    in_specs += [_row_spec(mods, layer + 1, 0, gof), _row_spec(mods, layer + 1, 1, gof)]
    args += [mods, mods]
    return pl.pallas_call(
        functools.partial(_postln_next_kernel, alpha=alpha, tm=tm, lat_rows=lat_rows),
        grid=(rows // tm,), in_specs=in_specs, out_specs=[tile, tile],
        out_shape=[jax.ShapeDtypeStruct((rows, d), F32), jax.ShapeDtypeStruct((rows, d), BF16)],
        compiler_params=_params(1), name="postln_next",
    )(*args)


def _rot_cols(w):
    shp = w.shape
    wr = w.reshape(shp[:-1] + (2, 2, ROPE_FREQ))
    return jnp.stack([-wr[..., 1, :], wr[..., 0, :]], axis=-2).reshape(shp)


def _rope_tables(batch, seq, ctx_rows):
    rows = seq // GRID_W
    row = jnp.repeat(jnp.arange(rows, dtype=F32), GRID_W)
    col = jnp.tile(jnp.arange(GRID_W, dtype=F32), rows)
    pos = jnp.stack([row, col], axis=-1)
    inv_freq = ROPE_BASE ** (-(jnp.arange(ROPE_FREQ, dtype=F32) * 2.0) / (ROPE_DIM // 2))
    ang = pos[:, :, None] * inv_freq
    expand = lambda t: jnp.broadcast_to(t[:, :, None, :], (seq, 2, 2, ROPE_FREQ)).reshape(seq, ROPE_DIM)
    cos = jnp.concatenate([jnp.tile(expand(jnp.cos(ang)), (batch, 1)), jnp.ones((ctx_rows, ROPE_DIM), F32)])
    sin = jnp.concatenate([jnp.tile(expand(jnp.sin(ang)), (batch, 1)), jnp.zeros((ctx_rows, ROPE_DIM), F32)])
    pad = lambda t: jnp.pad(t, ((0, 0), (0, LANE - ROPE_DIM)))
    cos, sin = pad(cos), pad(sin)
    return cos, sin, cos.T, sin.T


def kernel(x, c, ctx, c_ctx, w_ada, b_ada, w_in, conv_w, q_norm_g, w_uq, kv_norm_g, w_ukv, w_out, ln_g, ln_b):
    batch, seq, d = x.shape
    ctx_len = ctx.shape[1]
    depth = w_ada.shape[0]
    cw = conv_w.shape[-1]
    heads = (w_out.shape[1] - cw) // HEAD_DIM
    ql, kvl = q_norm_g.shape[-1], kv_norm_g.shape[-1]
    lat_rows, ctx_rows = batch * seq, batch * ctx_len
    m_all = lat_rows + ctx_rows
    alpha = (2.0 * depth) ** 0.25

    sec = [cw, cw, cw, cw, ql, kvl, ROPE_DIM, cw]
    offs = [0]
    for s in sec:
        offs.append(offs[-1] + s)
    w_in_bf = w_in.astype(BF16)
    col = lambda k: w_in_bf[:, :, offs[k]:offs[k + 1]]
    w_zm = col(7)
    w_lora = jnp.concatenate([w_in_bf[:, :, offs[4]:offs[7]], _rot_cols(col(6))], axis=-1)
    wq = w_uq.reshape(depth, ql, heads, HEAD_DIM + ROPE_DIM)
    wq = jnp.concatenate([wq, _rot_cols(wq[..., HEAD_DIM:])], axis=-1).reshape(depth, ql, heads * QK_PAD)
    wq_t = jnp.swapaxes(wq, 1, 2).astype(BF16)
    wkv = w_ukv.reshape(depth, kvl, heads, 2 * HEAD_DIM)
    wk = wkv[..., :HEAD_DIM].reshape(depth, kvl, heads * HEAD_DIM).astype(BF16)
    wv_t = jnp.swapaxes(wkv[..., HEAD_DIM:].reshape(depth, kvl, heads * HEAD_DIM), 1, 2).astype(BF16)
    tables = _rope_tables(batch, seq, ctx_rows)

    cc = jnp.zeros((SUBLANE, d), F32).at[:batch].set(c).at[batch].set(c_ctx)
    mods = _modulation(cc, w_ada, b_ada)[:, :batch + 1].reshape(depth, batch + 1, 3, 1, d)
    group_of = lambda i, tm: jnp.minimum((i * tm) // seq, batch)

    tm = _tile(math.gcd(seq, ctx_rows), (512, 256, 128))
    tm_small = _tile(math.gcd(seq, ctx_len), (256, 128))
    assert tm_small == ctx_len, "the context keys must form exactly one key chunk"
    assert w_out.shape[1] == 2 * cw, "the out-projection reads w_out as two row halves"
    x_src = (x.reshape(lat_rows, d), ctx.reshape(ctx_rows, d), 0)
    h = _modulate(x_src, mods, 0, m_all, lat_rows, group_of, tm)

    for layer in range(depth):
        last = layer == depth - 1
        rows = lat_rows if last else m_all
        tn = _tile(cw, (1024, 512, 256, 128))
        p_conv = _matmul(h, w_in_bf, layer, BF16, rows, tm, _tile(4 * cw, (2048, 1024, 512, 256, 128)),
                         "in_proj_conv", n_cols=4 * cw)
        p_zm = _matmul(h, w_zm, layer, BF16, rows, tm, tn, "in_proj_zm")
        p_lora = _matmul(h, w_lora, layer, F32, m_all, tm, w_lora.shape[-1], "in_proj_lora")
        q_t, k, v_t = _up_project(p_lora, tables, q_norm_g, kv_norm_g, wq_t, wk, wv_t, layer, heads, tm_small)
        y_lat = _attend_latent(q_t, k, v_t, batch, seq, ctx_len, heads,
                               _tile(seq, (2048, 1024, 512, 256, 128)), tm_small)
        y_ctx = y_lat if last else _attend_ctx(q_t, k, v_t, batch, seq, ctx_len, heads)
        y_conv, y_mla = _merge(p_conv, p_zm, y_lat, y_ctx, conv_w, layer, rows, lat_rows, seq, ctx_len,
                               tm_small, _tile(cw, (1024, 512, 256, 128)))
        out = _matmul2(y_conv, y_mla, w_out, layer, rows, tm, _tile(d, (1024, 512, 256, 128)), "out_proj")
        if last:
            x_all = _postln(x_src, out, mods, ln_g, ln_b, layer, rows, lat_rows, group_of, tm_small, alpha, True)
        else:
            x_all, h = _postln(x_src, out, mods, ln_g, ln_b, layer, rows, lat_rows, group_of, tm_small, alpha, False)
            x_src = (x_all, x_all, lat_rows // tm_small)
    return x_all.reshape(batch, seq, d)
```

```python
import functools
import math

import jax
import jax.numpy as jnp
from jax import lax
from jax.experimental import pallas as pl
from jax.experimental.pallas import tpu as pltpu

F32 = jnp.float32
BF16 = jnp.bfloat16

HEAD_DIM = 128
ROPE_DIM = 64
ROPE_FREQ = ROPE_DIM // 4
ROPE_BASE = 10000.0
GRID_W = 64
QK_PAD = 2 * HEAD_DIM
V_ROWS = HEAD_DIM + 16
EPS = 1e-6
ATTN_SCALE = 1.0 / math.sqrt(HEAD_DIM + ROPE_DIM)
Q_PRESCALE = ATTN_SCALE * math.log2(math.e)
ATTN_UNROLL = 9
LANE = 128
SUBLANE = 8
VMEM_LIMIT = 56 * 1024 * 1024


def _params(n_axes, vmem=VMEM_LIMIT):
    return pltpu.CompilerParams(dimension_semantics=("arbitrary",) * n_axes,
                                vmem_limit_bytes=vmem)


def _tile(n, prefs):
    for t in prefs:
        if n % t == 0:
            return t
    return n


def _silu(x):
    h = 0.5 * x
    return h + h * jnp.tanh(h)


def _mod_kernel(a_ref, w_ref, b_ref, o_ref):
    a = _silu(a_ref[...]).astype(BF16)
    o_ref[...] = jnp.dot(a, w_ref[...].astype(BF16), preferred_element_type=F32) + b_ref[...]


def _modulation(cc, w_ada, b_ada):
    depth, d, n = w_ada.shape
    tn = _tile(n, (512, 256, 128))
    return pl.pallas_call(
        _mod_kernel,
        grid=(depth, n // tn),
        in_specs=[pl.BlockSpec((SUBLANE, d), lambda l, j: (0, 0)),
                  pl.BlockSpec((None, d, tn), lambda l, j: (l, 0, j)),
                  pl.BlockSpec((None, 1, tn), lambda l, j: (l, 0, j))],
        out_specs=pl.BlockSpec((None, SUBLANE, tn), lambda l, j: (l, 0, j)),
        out_shape=jax.ShapeDtypeStruct((depth, SUBLANE, n), F32),
        compiler_params=_params(2),
        name="adaln_mod",
    )(cc, w_ada, b_ada.reshape(depth, 1, n))


def _row_spec(mods, layer, part, group_of_tile):
    d = mods.shape[-1]
    return pl.BlockSpec((None, None, None, 1, d),
                        lambda i: (layer, group_of_tile(i), part, 0, 0))


def _row_source_specs(tm, d, lat_rows, ctx_first_tile):
    n_lat = lat_rows // tm
    return [pl.BlockSpec((tm, d), lambda i: (jnp.minimum(i, n_lat - 1), 0)),
            pl.BlockSpec((tm, d), lambda i: (jnp.maximum(i - n_lat, 0) + ctx_first_tile, 0))]


def _pick_rows(xl_ref, xc_ref, tm, lat_rows):
    return jnp.where(pl.program_id(0) * tm < lat_rows, xl_ref[...], xc_ref[...])


def _modulate_kernel(xl_ref, xc_ref, sh_ref, sc_ref, h_ref, *, tm, lat_rows):
    x = _pick_rows(xl_ref, xc_ref, tm, lat_rows)
    h_ref[...] = (x * (1.0 + sc_ref[...]) + sh_ref[...]).astype(h_ref.dtype)


def _modulate(x_src, mods, layer, rows, lat_rows, group_of, tm):
    x_lat, x_ctx, ctx_first_tile = x_src
    d = x_lat.shape[1]
    gof = lambda i: group_of(i, tm)
    return pl.pallas_call(
        functools.partial(_modulate_kernel, tm=tm, lat_rows=lat_rows),
        grid=(rows // tm,),
        in_specs=_row_source_specs(tm, d, lat_rows, ctx_first_tile) + [
                  _row_spec(mods, layer, 0, gof),
                  _row_spec(mods, layer, 1, gof)],
        out_specs=pl.BlockSpec((tm, d), lambda i: (i, 0)),
        out_shape=jax.ShapeDtypeStruct((rows, d), BF16),
        compiler_params=_params(1),
        name="modulate",
    )(x_lat, x_ctx, mods, mods)


def _mm_kernel(a_ref, b_ref, o_ref):
    o_ref[...] = jnp.dot(a_ref[...], b_ref[...], preferred_element_type=F32).astype(o_ref.dtype)


def _matmul(a, w, layer, out_dtype, rows, tm, tn, name, n_cols=None):
    k = a.shape[1]
    n = n_cols or w.shape[2]
    return pl.pallas_call(
        _mm_kernel,
        grid=(n // tn, rows // tm),
        in_specs=[pl.BlockSpec((tm, k), lambda j, i: (i, 0)),
                  pl.BlockSpec((None, k, tn), lambda j, i: (layer, 0, j))],
        out_specs=pl.BlockSpec((tm, tn), lambda j, i: (i, j)),
        out_shape=jax.ShapeDtypeStruct((rows, n), out_dtype),
        compiler_params=_params(2),
        name=name,
    )(a, w)


def _mm2_kernel(a1_ref, a2_ref, w1_ref, w2_ref, o_ref, wb1_ref, wb2_ref):
    @pl.when(pl.program_id(1) == 0)
    def _():
        wb1_ref[...] = w1_ref[...].astype(BF16)
        wb2_ref[...] = w2_ref[...].astype(BF16)

    acc = jnp.dot(a1_ref[...], wb1_ref[...], preferred_element_type=F32)
    acc += jnp.dot(a2_ref[...], wb2_ref[...], preferred_element_type=F32)
    o_ref[...] = acc.astype(o_ref.dtype)


def _matmul2(a1, a2, w, layer, rows, tm, tn, name):
    k = a1.shape[1]
    n = w.shape[2]
    return pl.pallas_call(
        _mm2_kernel,
        grid=(n // tn, rows // tm),
        in_specs=[pl.BlockSpec((tm, k), lambda j, i: (i, 0)),
                  pl.BlockSpec((tm, k), lambda j, i: (i, 0)),
                  pl.BlockSpec((None, k, tn), lambda j, i: (layer, 0, j)),
                  pl.BlockSpec((None, k, tn), lambda j, i: (layer, 1, j))],
        out_specs=pl.BlockSpec((tm, tn), lambda j, i: (i, j)),
        out_shape=jax.ShapeDtypeStruct((rows, n), BF16),
        scratch_shapes=[pltpu.VMEM((k, tn), BF16), pltpu.VMEM((k, tn), BF16)],
        compiler_params=_params(2),
        name=name,
    )(a1, a2, w, w)


def _rope_half(x, cos, sin):
    return x * cos + pltpu.roll(x, ROPE_DIM, axis=1) * sin


def _rms(x, g):
    return (x * lax.rsqrt(jnp.mean(x * x, axis=-1, keepdims=True) + EPS)) * g


def _dot_nt(a, b):
    return lax.dot_general(a, b, (((1,), (1,)), ((), ())), preferred_element_type=F32)


def _up_kernel(p_ref, cos_ref, sin_ref, cos_t_ref, sin_t_ref, gq_ref, gkv_ref, wq_t_ref, wk_ref, wv_t_ref,
               q_t_ref, k_ref, v_t_ref, *, ql, kvl, heads):
    cq = _rms(p_ref[:, :ql].astype(F32), gq_ref[...]).astype(BF16)
    ckv = _rms(p_ref[:, ql:ql + kvl].astype(F32), gkv_ref[...]).astype(BF16)
    krp = _rope_half(p_ref[:, ql + kvl:].astype(F32), cos_ref[...], sin_ref[...]).astype(BF16)
    cos_t = cos_t_ref[...] * Q_PRESCALE
    sin_t = sin_t_ref[...] * Q_PRESCALE
    q_t = _dot_nt(wq_t_ref[...], cq)
    kn = jnp.dot(ckv, wk_ref[...], preferred_element_type=F32)
    for h in range(heads):
        lo = h * QK_PAD
        hi = q_t[lo + HEAD_DIM:lo + QK_PAD, :]
        q_t_ref[lo:lo + HEAD_DIM, :] = (q_t[lo:lo + HEAD_DIM, :] * Q_PRESCALE).astype(BF16)
        q_t_ref[lo + HEAD_DIM:lo + QK_PAD, :] = (
            hi * cos_t + pltpu.roll(hi, ROPE_DIM, axis=0) * sin_t).astype(BF16)
        k_ref[:, lo:lo + HEAD_DIM] = kn[:, h * HEAD_DIM:(h + 1) * HEAD_DIM].astype(BF16)
        k_ref[:, lo + HEAD_DIM:lo + QK_PAD] = krp
    v_t = _dot_nt(wv_t_ref[...], ckv)
    for h in range(heads):
        lo = h * V_ROWS
        v_t_ref[lo:lo + HEAD_DIM, :] = v_t[h * HEAD_DIM:(h + 1) * HEAD_DIM, :].astype(BF16)
        v_t_ref[lo + HEAD_DIM:lo + V_ROWS, :] = jnp.ones((V_ROWS - HEAD_DIM, v_t.shape[1]), BF16)


def _up_project(p_lora, tables, gq, gkv, wq_t, wk, wv_t, layer, heads, tm):
    m, pw = p_lora.shape
    ql, kvl = gq.shape[-1], gkv.shape[-1]
    depth = wq_t.shape[0]
    cos_r, sin_r, cos_t, sin_t = tables
    const = lambda i: (layer, 0, 0)
    return pl.pallas_call(
        functools.partial(_up_kernel, ql=ql, kvl=kvl, heads=heads),
        grid=(m // tm,),
        in_specs=[pl.BlockSpec((tm, pw), lambda i: (i, 0)),
                  pl.BlockSpec((tm, LANE), lambda i: (i, 0)),
                  pl.BlockSpec((tm, LANE), lambda i: (i, 0)),
                  pl.BlockSpec((LANE, tm), lambda i: (0, i)),
                  pl.BlockSpec((LANE, tm), lambda i: (0, i)),
                  pl.BlockSpec((None, 1, ql), const),
                  pl.BlockSpec((None, 1, kvl), const),
                  pl.BlockSpec((None, heads * QK_PAD, ql), const),
                  pl.BlockSpec((None, kvl, heads * HEAD_DIM), const),
                  pl.BlockSpec((None, heads * HEAD_DIM, kvl), const)],
        out_specs=[pl.BlockSpec((heads * QK_PAD, tm), lambda i: (0, i)),
                   pl.BlockSpec((tm, heads * QK_PAD), lambda i: (i, 0)),
                   pl.BlockSpec((None, heads * V_ROWS, tm), lambda i: (i, 0, 0))],
        out_shape=[jax.ShapeDtypeStruct((heads * QK_PAD, m), BF16),
                   jax.ShapeDtypeStruct((m, heads * QK_PAD), BF16),
                   jax.ShapeDtypeStruct((m // tm, heads * V_ROWS, tm), BF16)],
        compiler_params=_params(1),
        name="lora_up",
    )(p_lora, cos_r, sin_r, cos_t, sin_t, gq.reshape(depth, 1, ql), gkv.reshape(depth, 1, kvl), wq_t, wk, wv_t)


def _attn_kernel(q_t_ref, kl_ref, kc_ref, vl_ref, vc_ref, o_ref,
                 s0_ref, s1_ref, s2_ref, p0_ref, p1_ref, p2_ref, a0_ref, a1_ref, a2_ref,
                 c0_ref, c1_ref, c2_ref, m_ref, acc_ref, *, tk, unroll):
    n_lat = kl_ref.shape[0] // tk
    n = n_lat + 1
    s_buf, p_buf, a_buf = (s0_ref, s1_ref, s2_ref), (p0_ref, p1_ref, p2_ref), (a0_ref, a1_ref, a2_ref)
    c_buf = (c0_ref, c1_ref, c2_ref)
    q_t = q_t_ref[...]

    def k_chunk(c):
        if isinstance(c, int) and c == n_lat:
            return kc_ref[...]
        return kl_ref[pl.ds(pl.multiple_of(c * tk, tk), tk), :]

    def v_chunk(c):
        if isinstance(c, int) and c == n_lat:
            return vc_ref[0]
        return vl_ref[c]

    def scores(c, slot):
        s = jnp.dot(k_chunk(c), q_t, preferred_element_type=F32)
        s_buf[slot][...] = s
        c_buf[slot][...] = jnp.max(s, axis=0, keepdims=True)

    def softmax(slot):
        m_old = m_ref[...]
        m_new = jnp.maximum(m_old, c_buf[slot][...])
        alpha = jnp.exp2(m_old - m_new)
        m_ref[...] = m_new
        a_buf[slot][...] = alpha
        p_buf[slot][...] = jnp.exp2((s_buf[slot][...] - m_new).astype(BF16))

    def values(c, slot):
        acc_ref[...] = a_buf[slot][...] * acc_ref[...] + jnp.dot(
            v_chunk(c), p_buf[slot][...], preferred_element_type=F32)

    def stage(c, slot, with_scores=True, with_values=True):
        if with_scores:
            scores(c + 2, (slot + 2) % 3)
        if with_values:
            values(c - 2, (slot + 1) % 3)
        softmax(slot)

    m_ref[...] = jnp.full(m_ref.shape, -jnp.inf, F32)
    acc_ref[...] = jnp.zeros(acc_ref.shape, F32)
    scores(0, 0)
    scores(1, 1)
    for c in range(2):
        stage(c, c, with_scores=c + 2 < n, with_values=False)
    n_trips = max(n - 5, 0) // unroll

    def trip(j, carry):
        for r in range(unroll):
            stage(2 + unroll * j + r, (2 + r) % 3)
        return carry

    lax.fori_loop(0, n_trips, trip, 0)
    for c in range(2 + unroll * n_trips, n):
        stage(c, c % 3, with_scores=c + 2 < n)
    values(n - 2, (n - 2) % 3)
    values(n - 1, (n - 1) % 3)
    o_t = acc_ref[:HEAD_DIM, :] / acc_ref[HEAD_DIM:HEAD_DIM + 1, :]
    o_ref[...] = o_t.T.astype(o_ref.dtype)


def _attend_latent(q_t, k, v_t, batch, seq, ctx_len, heads, tq, tk):
    n_q = seq // tq
    ctx0 = batch * seq // ctx_len
    n_lat = seq // tk
    return pl.pallas_call(
        functools.partial(_attn_kernel, tk=tk, unroll=ATTN_UNROLL),
        grid=(batch, heads, n_q),
        in_specs=[pl.BlockSpec((QK_PAD, tq), lambda b, h, i: (h, b * n_q + i)),
                  pl.BlockSpec((seq, QK_PAD), lambda b, h, i: (b, h)),
                  pl.BlockSpec((ctx_len, QK_PAD), lambda b, h, i: (ctx0 + b, h)),
                  pl.BlockSpec((n_lat, V_ROWS, tk), lambda b, h, i: (b, h, 0)),
                  pl.BlockSpec((ctx_len // tk, V_ROWS, tk), lambda b, h, i: (ctx0 + b, h, 0))],
        out_specs=pl.BlockSpec((tq, HEAD_DIM), lambda b, h, i: (b * n_q + i, h)),
        out_shape=jax.ShapeDtypeStruct((batch * seq, heads * HEAD_DIM), BF16),
        scratch_shapes=[pltpu.VMEM((tk, tq), F32)] * 3 + [pltpu.VMEM((tk, tq), BF16)] * 3 + [
                        pltpu.VMEM((1, tq), F32)] * 6 + [
                        pltpu.VMEM((1, tq), F32), pltpu.VMEM((V_ROWS, tq), F32)],
        compiler_params=_params(3),
        name="attn_latent",
    )(q_t, k, k, v_t, v_t)


def _attn_ctx_kernel(q_t_ref, k_ref, v_t_ref, o_ref):
    s = jnp.dot(k_ref[...], q_t_ref[...], preferred_element_type=F32)
    p = jnp.exp2((s - jnp.max(s, axis=0, keepdims=True)).astype(BF16))
    o_t = jnp.dot(v_t_ref[0], p, preferred_element_type=F32)
    o_ref[...] = (o_t[:HEAD_DIM, :] / o_t[HEAD_DIM:HEAD_DIM + 1, :]).T.astype(o_ref.dtype)


def _attend_ctx(q_t, k, v_t, batch, seq, ctx_len, heads):
    ctx0 = batch * seq // ctx_len
    return pl.pallas_call(
        _attn_ctx_kernel,
        grid=(batch, heads),
        in_specs=[pl.BlockSpec((QK_PAD, ctx_len), lambda b, h: (h, ctx0 + b)),
                  pl.BlockSpec((ctx_len, QK_PAD), lambda b, h: (ctx0 + b, h)),
                  pl.BlockSpec((1, V_ROWS, ctx_len), lambda b, h: (ctx0 + b, h, 0))],
        out_specs=pl.BlockSpec((ctx_len, HEAD_DIM), lambda b, h: (b, h)),
        out_shape=jax.ShapeDtypeStruct((batch * ctx_len, heads * HEAD_DIM), BF16),
        compiler_params=_params(2),
        name="attn_ctx",
    )(q_t, k, v_t)


def _merge_kernel(u_ref, bg_ref, cg_ref, zc_ref, zm_ref, up_ref, cp_ref, un_ref, cn_ref,
                  ya_lat_ref, ya_ctx_ref, cw_ref, yc_ref, ym_ref, *, tm, lat_rows, seq, ctx_len):
    i = pl.program_id(0)
    row0 = i * tm
    row1 = row0 + tm
    is_start = jnp.where(row0 < lat_rows, row0 % seq == 0, (row0 - lat_rows) % ctx_len == 0)
    is_end = jnp.where(row1 <= lat_rows, row1 % seq == 0, (row1 - lat_rows) % ctx_len == 0)
    v = (cg_ref[...] * u_ref[...]).astype(F32)
    v_before = (cp_ref[SUBLANE - 1:SUBLANE, :] * up_ref[SUBLANE - 1:SUBLANE, :]).astype(F32)
    v_after = (cn_ref[0:1, :] * un_ref[0:1, :]).astype(F32)
    v_before = jnp.where(is_start, 0.0, v_before)
    v_after = jnp.where(is_end, 0.0, v_after)
    row = lax.broadcasted_iota(jnp.int32, v.shape, 0)
    v_prev = jnp.where(row == 0, v_before, pltpu.roll(v, 1, axis=0))
    v_next = jnp.where(row == tm - 1, v_after, pltpu.roll(v, tm - 1, axis=0))
    cw = cw_ref[...]
    y = (v_prev * cw[0:1, :] + v * cw[1:2, :] + v_next * cw[2:3, :]).astype(BF16)
    yc_ref[...] = bg_ref[...] * y * _silu(zc_ref[...])
    ya = jnp.where(row0 < lat_rows, ya_lat_ref[...], ya_ctx_ref[...])
    ym_ref[...] = ya * _silu(zm_ref[...])


def _merge(p_conv, p_zm, y_lat, y_ctx, conv_w, layer, rows, lat_rows, seq, ctx_len, tm, tc):
    m_all = p_conv.shape[0]
    cw = conv_w.shape[-1]
    nc = cw // tc
    hb = tm // SUBLANE
    last_hb = m_all // SUBLANE - 1
    n_lat = lat_rows // tm
    ya_lat = pl.BlockSpec((tm, tc), lambda i, j: (jnp.minimum(i, n_lat - 1), j))
    ya_ctx = pl.BlockSpec((tm, tc), lambda i, j: (jnp.maximum(i - n_lat, 0), j))
    main = lambda s: pl.BlockSpec((tm, tc), lambda i, j: (i, s * nc + j))
    prev = lambda s: pl.BlockSpec((SUBLANE, tc), lambda i, j: (jnp.maximum(i * hb - 1, 0), s * nc + j))
    nxt = lambda s: pl.BlockSpec((SUBLANE, tc), lambda i, j: (jnp.minimum((i + 1) * hb, last_hb), s * nc + j))
    out = pl.BlockSpec((tm, tc), lambda i, j: (i, j))
    return pl.pallas_call(
        functools.partial(_merge_kernel, tm=tm, lat_rows=lat_rows, seq=seq, ctx_len=ctx_len),
        grid=(rows // tm, nc),
        in_specs=[main(0), main(1), main(2), main(3), out,
                  prev(0), prev(2), nxt(0), nxt(2),
                  ya_lat, ya_ctx,
                  pl.BlockSpec((None, 3, tc), lambda i, j: (layer, 0, j))],
        out_specs=[out, out],
        out_shape=[jax.ShapeDtypeStruct((rows, cw), BF16), jax.ShapeDtypeStruct((rows, cw), BF16)],
        compiler_params=_params(2),
        name="conv_merge",
    )(p_conv, p_conv, p_conv, p_conv, p_zm, p_conv, p_conv, p_conv, p_conv, y_lat, y_ctx, conv_w)


def _layernorm(z, g, b):
    mu = jnp.mean(z, axis=-1, keepdims=True)
    zc = z - mu
    var = jnp.mean(zc * zc, axis=-1, keepdims=True)
    return zc * lax.rsqrt(var + EPS) * g + b


def _postln_next_kernel(xl_ref, xc_ref, o_ref, gate_ref, g_ref, b_ref, sh_ref, sc_ref, xn_ref, h_ref,
                        *, alpha, tm, lat_rows):
    x = _pick_rows(xl_ref, xc_ref, tm, lat_rows)
    xn = _layernorm(alpha * x + gate_ref[...] * o_ref[...], g_ref[...], b_ref[...])
    xn_ref[...] = xn
    h_ref[...] = (xn * (1.0 + sc_ref[...]) + sh_ref[...]).astype(h_ref.dtype)


def _postln_last_kernel(xl_ref, xc_ref, o_ref, gate_ref, g_ref, b_ref, xn_ref, *, alpha, tm, lat_rows):
    x = _pick_rows(xl_ref, xc_ref, tm, lat_rows)
    xn_ref[...] = _layernorm(alpha * x + gate_ref[...] * o_ref[...], g_ref[...], b_ref[...])


def _postln(x_src, out, mods, ln_g, ln_b, layer, rows, lat_rows, group_of, tm, alpha, last):
    x_lat, x_ctx, ctx_first_tile = x_src
    d = x_lat.shape[1]
    depth = ln_g.shape[0]
    gof = lambda i: group_of(i, tm)
    tile = pl.BlockSpec((tm, d), lambda i: (i, 0))
    vec = pl.BlockSpec((None, 1, d), lambda i: (layer, 0, 0))
    in_specs = _row_source_specs(tm, d, lat_rows, ctx_first_tile) + [
        tile, _row_spec(mods, layer, 2, gof), vec, vec]
    args = [x_lat, x_ctx, out, mods, ln_g.reshape(depth, 1, d), ln_b.reshape(depth, 1, d)]
    if last:
        return pl.pallas_call(
            functools.partial(_postln_last_kernel, alpha=alpha, tm=tm, lat_rows=lat_rows),
            grid=(rows // tm,), in_specs=in_specs, out_specs=tile,
            out_shape=jax.ShapeDtypeStruct((rows, d), F32),
            compiler_params=_params(1), name="postln_last",
        )(*args)
    in_specs += [_row_spec(mods, layer + 1, 0, gof), _row_spec(mods, layer + 1, 1, gof)]
    args += [mods, mods]
    return pl.pallas_call(
        functools.partial(_postln_next_kernel, alpha=alpha, tm=tm, lat_rows=lat_rows),
        grid=(rows // tm,), in_specs=in_specs, out_specs=[tile, tile],
        out_shape=[jax.ShapeDtypeStruct((rows, d), F32), jax.ShapeDtypeStruct((rows, d), BF16)],
        compiler_params=_params(1), name="postln_next",
    )(*args)


def _rot_cols(w):
    shp = w.shape
    wr = w.reshape(shp[:-1] + (2, 2, ROPE_FREQ))
    return jnp.stack([-wr[..., 1, :], wr[..., 0, :]], axis=-2).reshape(shp)


def _rope_tables(batch, seq, ctx_rows):
    rows = seq // GRID_W
    row = jnp.repeat(jnp.arange(rows, dtype=F32), GRID_W)
    col = jnp.tile(jnp.arange(GRID_W, dtype=F32), rows)
    pos = jnp.stack([row, col], axis=-1)
    inv_freq = ROPE_BASE ** (-(jnp.arange(ROPE_FREQ, dtype=F32) * 2.0) / (ROPE_DIM // 2))
    ang = pos[:, :, None] * inv_freq
    expand = lambda t: jnp.broadcast_to(t[:, :, None, :], (seq, 2, 2, ROPE_FREQ)).reshape(seq, ROPE_DIM)
    cos = jnp.concatenate([jnp.tile(expand(jnp.cos(ang)), (batch, 1)), jnp.ones((ctx_rows, ROPE_DIM), F32)])
    sin = jnp.concatenate([jnp.tile(expand(jnp.sin(ang)), (batch, 1)), jnp.zeros((ctx_rows, ROPE_DIM), F32)])
    pad = lambda t: jnp.pad(t, ((0, 0), (0, LANE - ROPE_DIM)))
    cos, sin = pad(cos), pad(sin)
    return cos, sin, cos.T, sin.T


def kernel(x, c, ctx, c_ctx, w_ada, b_ada, w_in, conv_w, q_norm_g, w_uq, kv_norm_g, w_ukv, w_out, ln_g, ln_b):
    batch, seq, d = x.shape
    ctx_len = ctx.shape[1]
    depth = w_ada.shape[0]
    cw = conv_w.shape[-1]
    heads = (w_out.shape[1] - cw) // HEAD_DIM
    ql, kvl = q_norm_g.shape[-1], kv_norm_g.shape[-1]
    lat_rows, ctx_rows = batch * seq, batch * ctx_len
    m_all = lat_rows + ctx_rows
    alpha = (2.0 * depth) ** 0.25

    sec = [cw, cw, cw, cw, ql, kvl, ROPE_DIM, cw]
    offs = [0]
    for s in sec:
        offs.append(offs[-1] + s)
    w_in_bf = w_in.astype(BF16)
    col = lambda k: w_in_bf[:, :, offs[k]:offs[k + 1]]
    w_zm = col(7)
    w_lora = jnp.concatenate([w_in_bf[:, :, offs[4]:offs[7]], _rot_cols(col(6))], axis=-1)
    wq = w_uq.reshape(depth, ql, heads, HEAD_DIM + ROPE_DIM)
    wq = jnp.concatenate([wq, _rot_cols(wq[..., HEAD_DIM:])], axis=-1).reshape(depth, ql, heads * QK_PAD)
    wq_t = jnp.swapaxes(wq, 1, 2).astype(BF16)
    wkv = w_ukv.reshape(depth, kvl, heads, 2 * HEAD_DIM)
    wk = wkv[..., :HEAD_DIM].reshape(depth, kvl, heads * HEAD_DIM).astype(BF16)
    wv_t = jnp.swapaxes(wkv[..., HEAD_DIM:].reshape(depth, kvl, heads * HEAD_DIM), 1, 2).astype(BF16)
    tables = _rope_tables(batch, seq, ctx_rows)

    cc = jnp.zeros((SUBLANE, d), F32).at[:batch].set(c).at[batch].set(c_ctx)
    mods = _modulation(cc, w_ada, b_ada)[:, :batch + 1].reshape(depth, batch + 1, 3, 1, d)
    group_of = lambda i, tm: jnp.minimum((i * tm) // seq, batch)

    tm = _tile(math.gcd(seq, ctx_rows), (512, 256, 128))
    tm_small = _tile(math.gcd(seq, ctx_len), (256, 128))
    assert tm_small == ctx_len, "the context keys must form exactly one key chunk"
    assert w_out.shape[1] == 2 * cw, "the out-projection reads w_out as two row halves"
    x_src = (x.reshape(lat_rows, d), ctx.reshape(ctx_rows, d), 0)
    h = _modulate(x_src, mods, 0, m_all, lat_rows, group_of, tm)

    for layer in range(depth):
        last = layer == depth - 1
        rows = lat_rows if last else m_all
        tn = _tile(cw, (1024, 512, 256, 128))
        p_conv = _matmul(h, w_in_bf, layer, BF16, rows, tm, _tile(4 * cw, (2048, 1024, 512, 256, 128)),
                         "in_proj_conv", n_cols=4 * cw)
        p_zm = _matmul(h, w_zm, layer, BF16, rows, tm, tn, "in_proj_zm")
        p_lora = _matmul(h, w_lora, layer, BF16, m_all, tm, w_lora.shape[-1], "in_proj_lora")
        q_t, k, v_t = _up_project(p_lora, tables, q_norm_g, kv_norm_g, wq_t, wk, wv_t, layer, heads, tm_small)
        y_lat = _attend_latent(q_t, k, v_t, batch, seq, ctx_len, heads,
                               _tile(seq, (2048, 1024, 512, 256, 128)), tm_small)
        y_ctx = y_lat if last else _attend_ctx(q_t, k, v_t, batch, seq, ctx_len, heads)
        y_conv, y_mla = _merge(p_conv, p_zm, y_lat, y_ctx, conv_w, layer, rows, lat_rows, seq, ctx_len,
                               tm_small, _tile(cw, (1024, 512, 256, 128)))
        out = _matmul2(y_conv, y_mla, w_out, layer, rows, tm, _tile(d, (1024, 512, 256, 128)), "out_proj")
        if last:
            x_all = _postln(x_src, out, mods, ln_g, ln_b, layer, rows, lat_rows, group_of, tm_small, alpha, True)
        else:
            x_all, h = _postln(x_src, out, mods, ln_g, ln_b, layer, rows, lat_rows, group_of, tm_small, alpha, False)
            x_src = (x_all, x_all, lat_rows // tm_small)
    return x_all.reshape(batch, seq, d)
```
